```python
import math
import jax, jax.numpy as jnp
from jax import lax
import numpy as np

D_MODEL = 1024
BATCH = 8
SEQ = 4096
DEPTH = 1

MIX_WIDTH = D_MODEL
ATT_WIDTH = MIX_WIDTH // 2
CONV_WIDTH = MIX_WIDTH - ATT_WIDTH
ATT_HEADS = 4
ATT_DH = ATT_WIDTH // (2 * ATT_HEADS)
ATT_VH = 2 * ATT_DH
CONV_GROUPS = 8
CONV_K = 3
D_FF = ((8 * D_MODEL // 3 + 255) // 256) * 256
N_BUCKETS = 32
MAX_DISTANCE = 128
Q_BLOCK = 128
EPS = 1e-6
IN_WIDTH = 3 * ATT_WIDTH + 3 * CONV_WIDTH

kernel_name = 'hybrid_diffattn_shortconv_encoder'


def rms_norm(x, g):
    xf = x.astype(jnp.float32)
    y = xf * lax.rsqrt(jnp.mean(xf * xf, axis=-1, keepdims=True) + EPS)
    return (y * g.astype(jnp.float32)).astype(x.dtype)


def t5_bucket(rel):
    n_half = N_BUCKETS // 2
    max_exact = n_half // 2
    ret = jnp.where(rel > 0, n_half, 0).astype(jnp.int32)
    n = jnp.abs(rel).astype(jnp.int32)
    nf = jnp.maximum(n, 1).astype(jnp.float32)
    large = max_exact + (jnp.log(nf / max_exact) / math.log(MAX_DISTANCE / max_exact)
                         * (n_half - max_exact)).astype(jnp.int32)
    large = jnp.minimum(large, n_half - 1)
    return ret + jnp.where(n < max_exact, n, large)


def lambda_init_fn(layer_idx):
    return 0.8 - 0.6 * math.exp(-0.3 * layer_idx)


def diff_attention(q, k, v, lam, rel_bias):
    B, S, _ = q.shape
    q = q.reshape(B, S, ATT_HEADS, 2, ATT_DH) * (ATT_DH ** -0.5)
    k = k.reshape(B, S, ATT_HEADS, 2, ATT_DH)
    v = v.reshape(B, S, ATT_HEADS, ATT_VH)
    n_blk = S // Q_BLOCK
    q_blocks = q.reshape(B, n_blk, Q_BLOCK, ATT_HEADS, 2, ATT_DH).transpose(1, 0, 2, 3, 4, 5)
    starts = jnp.arange(n_blk, dtype=jnp.int32) * Q_BLOCK
    k_pos = jnp.arange(S, dtype=jnp.int32)

    def block(args):
        q_blk, start = args
        q_pos = start + jnp.arange(Q_BLOCK, dtype=jnp.int32)
        bucket = t5_bucket(k_pos[None, :] - q_pos[:, None])
        bias = jnp.transpose(rel_bias[bucket], (2, 0, 1)).astype(jnp.float32)
        s = jnp.einsum('bqhcd,bkhcd->bhcqk', q_blk, k,
                       preferred_element_type=jnp.float32) + bias[None, :, None]
        p = jax.nn.softmax(s, axis=-1)
        a = p[:, :, 0] - lam * p[:, :, 1]
        return jnp.einsum('bhqk,bkhe->bqhe', a.astype(v.dtype), v)

    o = lax.map(block, (q_blocks, starts))
    return o.transpose(1, 0, 2, 3, 4).reshape(B, S, ATT_HEADS, ATT_VH)


def short_conv(u, w):
    up = jnp.pad(u, ((0, 0), (1, 1), (0, 0)))
    return up[:, :-2] * w[0] + up[:, 1:-1] * w[1] + up[:, 2:] * w[2]


def setup_inputs(seed: int = 0) -> dict:
    key = jax.random.key(seed)
    ks = jax.random.split(key, 20)
    f32 = jnp.float32
    nrm = lambda k, shape, scale: jax.random.normal(k, shape, f32) * scale
    return {
        'x': jax.random.normal(ks[0], (BATCH, SEQ, D_MODEL), f32),
        'norm_mix_g': 1.0 + nrm(ks[1], (DEPTH, D_MODEL), 0.02),
        'w_in': nrm(ks[2], (DEPTH, D_MODEL, IN_WIDTH), D_MODEL ** -0.5),
        'lambda_q1': nrm(ks[3], (DEPTH, ATT_DH), 0.1),
        'lambda_k1': nrm(ks[4], (DEPTH, ATT_DH), 0.1),
        'lambda_q2': nrm(ks[5], (DEPTH, ATT_DH), 0.1),
        'lambda_k2': nrm(ks[6], (DEPTH, ATT_DH), 0.1),
        'subln_g': 1.0 + nrm(ks[7], (DEPTH, ATT_VH), 0.02),
        'rel_bias': nrm(ks[8], (N_BUCKETS, ATT_HEADS), 0.5),
        'conv_w': nrm(ks[9], (DEPTH, CONV_K, CONV_WIDTH), CONV_K ** -0.5),
        'w_out': nrm(ks[10], (DEPTH, MIX_WIDTH, D_MODEL), MIX_WIDTH ** -0.5),
        'norm_ffn_g': 1.0 + nrm(ks[11], (DEPTH, D_MODEL), 0.02),
        'w_gate': nrm(ks[12], (DEPTH, D_MODEL, D_FF), D_MODEL ** -0.5),
        'w_up': nrm(ks[13], (DEPTH, D_MODEL, D_FF), D_MODEL ** -0.5),
        'w_down': nrm(ks[14], (DEPTH, D_FF, D_MODEL), D_FF ** -0.5),
        'final_g': 1.0 + nrm(ks[15], (D_MODEL,), 0.02),
    }


def reference(x, norm_mix_g, w_in, lambda_q1, lambda_k1, lambda_q2, lambda_k2, subln_g,
              rel_bias, conv_w, w_out, norm_ffn_g, w_gate, w_up, w_down, final_g):
    B, S, _ = x.shape
    for l in range(DEPTH):
        u = rms_norm(x, norm_mix_g[l])
        proj = jnp.einsum('bsd,de->bse', u, w_in[l])
        q, k, v, bg, cg, hv = jnp.split(
            proj, np.cumsum([ATT_WIDTH, ATT_WIDTH, ATT_WIDTH, CONV_WIDTH, CONV_WIDTH]).tolist(), axis=-1)

        lam_init = lambda_init_fn(l)
        lam = (jnp.exp(jnp.sum(lambda_q1[l].astype(jnp.float32) * lambda_k1[l].astype(jnp.float32)))
               - jnp.exp(jnp.sum(lambda_q2[l].astype(jnp.float32) * lambda_k2[l].astype(jnp.float32)))
               + lam_init)
        att = diff_attention(q, k, v, lam, rel_bias)
        att = rms_norm(att, subln_g[l]) * (1.0 - lam_init)
        att = att.reshape(B, S, ATT_WIDTH)

        conv = bg * short_conv(cg * hv, conv_w[l])

        mixed = jnp.concatenate([att, conv.astype(att.dtype)], axis=-1)
        x = x + jnp.einsum('bse,ed->bsd', mixed, w_out[l])

        z = rms_norm(x, norm_ffn_g[l])
        hid = jax.nn.silu(jnp.einsum('bsd,df->bsf', z, w_gate[l])) * jnp.einsum('bsd,df->bsf', z, w_up[l])
        x = x + jnp.einsum('bsf,fd->bsd', hid, w_down[l])
    return rms_norm(x, final_g)
```

```python
import functools
import math

import jax
import jax.numpy as jnp
from jax import lax
from jax.experimental import pallas as pl
from jax.experimental.pallas import tpu as pltpu

F32 = jnp.float32
BF16 = jnp.bfloat16

ATT_HEADS = 4
ATT_DH = 64
ATT_VH = 2 * ATT_DH
N_BUCKETS = 32
MAX_DISTANCE = 128
EPS = 1e-6
LAMBDA_INIT = 0.8 - 0.6 * math.exp(-0.3 * 0)
LOG2E = 1.4426950408889634

LANES = 128
BF16_SUBLANES = 16
VMEM_LIMIT_CAP = 56 * 1024 * 1024

ROW_TILE = 512
ATT_TILE = 256
PROJ_COL_CHUNK = 512
FFN_COL_CHUNK = 256
M_INIT = -1e30


def _vmem_limit(nbytes):
    return int(min(max(2 * nbytes, 16 * 1024 * 1024), VMEM_LIMIT_CAP))


def _t5_bucket(rel):
    n_half = N_BUCKETS // 2
    max_exact = n_half // 2
    ret = jnp.where(rel > 0, n_half, 0).astype(jnp.int32)
    n = jnp.abs(rel).astype(jnp.int32)
    nf = jnp.maximum(n, 1).astype(jnp.float32)
    large = max_exact + (jnp.log(nf / max_exact) / math.log(MAX_DISTANCE / max_exact)
                         * (n_half - max_exact)).astype(jnp.int32)
    large = jnp.minimum(large, n_half - 1)
    return ret + jnp.where(n < max_exact, n, large)


def _bias_tiles(rel_bias, tile):
    assert tile >= MAX_DISTANCE
    kk = jnp.arange(tile, dtype=jnp.int32)[:, None]
    qq = jnp.arange(tile, dtype=jnp.int32)[None, :]
    tiles = []
    for dj in (-2, -1, 0, 1, 2):
        rel = dj * tile + kk - qq
        tiles.append(rel_bias[_t5_bucket(rel)])
    b = jnp.stack(tiles, axis=0)
    return jnp.transpose(b, (3, 0, 1, 2)).astype(F32) * LOG2E


def _in_proj_kernel(x_ref, g_ref, w_ref, o_ref, *, q_cols, q_scale):
    x = x_ref[...]
    ms = jnp.mean(x * x, axis=-1, keepdims=True)
    u = (x * lax.rsqrt(ms + EPS) * g_ref[...]).astype(BF16)
    n_out = o_ref.shape[1]
    for n0 in range(0, n_out, PROJ_COL_CHUNK):
        r = jnp.dot(u, w_ref[:, n0:n0 + PROJ_COL_CHUNK], preferred_element_type=F32)
        if n0 < q_cols:
            r = r * q_scale
        o_ref[:, n0:n0 + PROJ_COL_CHUNK] = r.astype(BF16)


def _in_proj(x2d, g, w_bf16, *, q_cols, q_scale):
    tokens, d = x2d.shape
    n_out = w_bf16.shape[1]
    assert tokens % ROW_TILE == 0 and q_cols % PROJ_COL_CHUNK == 0 and n_out % PROJ_COL_CHUNK == 0
    est = 2 * ROW_TILE * d * 4 + 2 * d * n_out * 2 + 2 * ROW_TILE * n_out * 2 + ROW_TILE * (d * 2 + PROJ_COL_CHUNK * 8)
    return pl.pallas_call(
        functools.partial(_in_proj_kernel, q_cols=q_cols, q_scale=q_scale),
        grid=(tokens // ROW_TILE,),
        in_specs=[
            pl.BlockSpec((ROW_TILE, d), lambda i: (i, 0)),
            pl.BlockSpec((1, d), lambda i: (0, 0)),
            pl.BlockSpec((d, n_out), lambda i: (0, 0)),
        ],
        out_specs=pl.BlockSpec((ROW_TILE, n_out), lambda i: (i, 0)),
        out_shape=jax.ShapeDtypeStruct((tokens, n_out), BF16),
        compiler_params=pltpu.CompilerParams(
            dimension_semantics=("arbitrary",), vmem_limit_bytes=_vmem_limit(est)),
        name="in_proj",
    )(x2d, g, w_bf16)


def _diff_attn_kernel(lamv_ref, q_ref, k_ref, v_ref, bias_ref, g_ref, o_ref, vt_scr, m_scr, acc_scr,
                      *, n_chunks):
    tile = ATT_TILE
    i = pl.program_id(2)

    @pl.when(i == 0)
    def _():
        for j in range(n_chunks):
            vc = v_ref[0, j * tile:(j + 1) * tile, :].astype(F32)
            vt_scr[j, 0:ATT_VH, :] = vc.T.astype(BF16)
            vt_scr[j, ATT_VH:ATT_VH + BF16_SUBLANES, :] = jnp.ones((BF16_SUBLANES, tile), BF16)

    qt = q_ref[0].astype(F32).T
    row = lax.broadcasted_iota(jnp.int32, qt.shape, 0)
    qts = (jnp.where(row < ATT_DH, qt, 0.0).astype(BF16),
           jnp.where(row >= ATT_DH, qt, 0.0).astype(BF16))

    m_scr[...] = jnp.full(m_scr.shape, M_INIT, F32)
    acc_scr[...] = jnp.zeros(acc_scr.shape, F32)

    def chunk(j, carry):
        kc = k_ref[0, pl.ds(pl.multiple_of(j * tile, tile), tile), :]
        bias = bias_ref[0, jnp.clip(j - i, -2, 2) + 2]
        vt = vt_scr[j]
        for c in range(2):
            s = jnp.dot(kc, qts[c], preferred_element_type=F32) + bias
            m_old = m_scr[c]
            m_new = jnp.maximum(m_old, jnp.max(s, axis=0, keepdims=True))
            p = jnp.exp2(s - m_new).astype(BF16)
            pv = jnp.dot(vt, p, preferred_element_type=F32)
            acc_scr[c] = jnp.exp2(m_old - m_new) * acc_scr[c] + pv
            m_scr[c] = m_new
        return carry

    lax.fori_loop(0, n_chunks, chunk, 0)

    lv = lamv_ref[...]
    lam = (jnp.exp(jnp.sum(lv[0:1] * lv[1:2], axis=-1, keepdims=True))
           - jnp.exp(jnp.sum(lv[2:3] * lv[3:4], axis=-1, keepdims=True)) + LAMBDA_INIT)
    a1 = acc_scr[0]
    a2 = acc_scr[1]
    o = a1[0:ATT_VH] / a1[ATT_VH:ATT_VH + 1] - lam * (a2[0:ATT_VH] / a2[ATT_VH:ATT_VH + 1])
    ms = jnp.mean(o * o, axis=0, keepdims=True)
    y = o * lax.rsqrt(ms + EPS) * g_ref[...] * (1.0 - LAMBDA_INIT)
    o_ref[0] = y.T.astype(BF16)


def _diff_attn(proj3d, lamv, bias_tiles, subln_col):
    batch, seq, _ = proj3d.shape
    tile = ATT_TILE
    assert seq % tile == 0 and 2 * ATT_DH == LANES and ATT_VH == LANES
    n_chunks = seq // tile
    acc_rows = ATT_VH + BF16_SUBLANES
    k_blk = ATT_HEADS
    v_blk = 2 * ATT_HEADS
    est = (4 * seq * LANES * 2 + 2 * 5 * tile * tile * 4 + n_chunks * acc_rows * tile * 2
           + 2 * acc_rows * tile * 4 + 8 * tile * tile * 4)
    return pl.pallas_call(
        functools.partial(_diff_attn_kernel, n_chunks=n_chunks),
        grid=(batch, ATT_HEADS, n_chunks),
        in_specs=[
            pl.BlockSpec((4, ATT_DH), lambda b, h, i: (0, 0)),
            pl.BlockSpec((1, tile, LANES), lambda b, h, i: (b, i, h)),
            pl.BlockSpec((1, seq, LANES), lambda b, h, i: (b, 0, k_blk + h)),
            pl.BlockSpec((1, seq, LANES), lambda b, h, i: (b, 0, v_blk + h)),
            pl.BlockSpec((1, 5, tile, tile), lambda b, h, i: (h, 0, 0, 0)),
            pl.BlockSpec((ATT_VH, 1), lambda b, h, i: (0, 0)),
        ],
        out_specs=pl.BlockSpec((1, tile, LANES), lambda b, h, i: (b, i, h)),
        out_shape=jax.ShapeDtypeStruct((batch, seq, ATT_HEADS * ATT_VH), BF16),
        scratch_shapes=[
            pltpu.VMEM((n_chunks, acc_rows, tile), BF16),
            pltpu.VMEM((2, 1, tile), F32),
            pltpu.VMEM((2, acc_rows, tile), F32),
        ],
        compiler_params=pltpu.CompilerParams(
            dimension_semantics=("arbitrary", "arbitrary", "arbitrary"),
            vmem_limit_bytes=_vmem_limit(est)),
        name="diff_attn",
    )(lamv, proj3d, proj3d, proj3d, bias_tiles, subln_col)


def _out_proj_kernel(x_ref, att_ref, b_ref, c_ref, h_ref, cp_ref, hp_ref, cn_ref, hn_ref, cw_ref, w_ref,
                     o_ref, gpad_scr, *, tiles_per_seq):
    tm = x_ref.shape[0]
    att_w = att_ref.shape[1]
    i = pl.program_id(0)
    pos = i % tiles_per_seq
    g = c_ref[...].astype(F32) * h_ref[...].astype(F32)
    g_prev = (cp_ref[...].astype(F32) * hp_ref[...].astype(F32))[BF16_SUBLANES - 1:BF16_SUBLANES]
    g_next = (cn_ref[...].astype(F32) * hn_ref[...].astype(F32))[0:1]
    g_prev = jnp.where(pos == 0, 0.0, g_prev)
    g_next = jnp.where(pos == tiles_per_seq - 1, 0.0, g_next)
    gpad_scr[7:8, :] = g_prev
    gpad_scr[8:8 + tm, :] = g
    gpad_scr[8 + tm:9 + tm, :] = g_next
    cw = cw_ref[...]
    conv = b_ref[...].astype(F32) * (cw[0:1] * gpad_scr[7:7 + tm, :] + cw[1:2] * g
                                     + cw[2:3] * gpad_scr[9:9 + tm, :])
    conv = conv.astype(BF16)
    att = att_ref[...]
    for n0 in range(0, o_ref.shape[1], PROJ_COL_CHUNK):
        cols = slice(n0, n0 + PROJ_COL_CHUNK)
        y = jnp.dot(att, w_ref[0:att_w, cols], preferred_element_type=F32)
        y = y + jnp.dot(conv, w_ref[att_w:, cols], preferred_element_type=F32)
        o_ref[:, cols] = x_ref[:, cols] + y


def _out_proj(x2d, att2d, proj2d, conv_w, w_bf16, *, seq):
    tokens, d = x2d.shape
    att_w = att2d.shape[1]
    conv_width = conv_w.shape[1]
    tm = ROW_TILE
    assert seq % tm == 0 and att_w == conv_width and d % PROJ_COL_CHUNK == 0
    tiles_per_seq = seq // tm
    blk0 = (3 * att_w) // conv_width
    halo = BF16_SUBLANES
    rows_per_tile = tm // halo
    last_halo = tokens // halo - 1
    est = (4 * tm * d * 4 + 8 * tm * conv_width * 2 + 2 * (att_w + conv_width) * d * 2
           + (tm + 16) * conv_width * 4 + 6 * tm * conv_width * 4)
    return pl.pallas_call(
        functools.partial(_out_proj_kernel, tiles_per_seq=tiles_per_seq),
        grid=(tokens // tm,),
        in_specs=[
            pl.BlockSpec((tm, d), lambda i: (i, 0)),
            pl.BlockSpec((tm, att_w), lambda i: (i, 0)),
            pl.BlockSpec((tm, conv_width), lambda i: (i, blk0)),
            pl.BlockSpec((tm, conv_width), lambda i: (i, blk0 + 1)),
            pl.BlockSpec((tm, conv_width), lambda i: (i, blk0 + 2)),
            pl.BlockSpec((halo, conv_width), lambda i: (jnp.maximum(i * rows_per_tile - 1, 0), blk0 + 1)),
            pl.BlockSpec((halo, conv_width), lambda i: (jnp.maximum(i * rows_per_tile - 1, 0), blk0 + 2)),
            pl.BlockSpec((halo, conv_width),
                         lambda i: (jnp.minimum((i + 1) * rows_per_tile, last_halo), blk0 + 1)),
            pl.BlockSpec((halo, conv_width),
                         lambda i: (jnp.minimum((i + 1) * rows_per_tile, last_halo), blk0 + 2)),
            pl.BlockSpec(conv_w.shape, lambda i: (0, 0)),
            pl.BlockSpec(w_bf16.shape, lambda i: (0, 0)),
        ],
        out_specs=pl.BlockSpec((tm, d), lambda i: (i, 0)),
        out_shape=jax.ShapeDtypeStruct((tokens, d), F32),
        scratch_shapes=[pltpu.VMEM((tm + 16, conv_width), F32)],
        compiler_params=pltpu.CompilerParams(
            dimension_semantics=("arbitrary",), vmem_limit_bytes=_vmem_limit(est)),
        name="out_proj",
    )(x2d, att2d, proj2d, proj2d, proj2d, proj2d, proj2d, proj2d, proj2d, conv_w, w_bf16)


def _ffn_kernel(x_ref, g_ref, wg_ref, wu_ref, wd_ref, fg_ref, o_ref, z_scr, hid_scr):
    d = x_ref.shape[1]
    d_ff = wg_ref.shape[1]
    x = x_ref[...]
    ms = jnp.mean(x * x, axis=-1, keepdims=True)
    z_scr[...] = (x * lax.rsqrt(ms + EPS) * g_ref[...]).astype(BF16)
    for f0 in range(0, d_ff, FFN_COL_CHUNK):
        cols = slice(f0, f0 + FFN_COL_CHUNK)
        z = z_scr[...]
        gate = jnp.dot(z, wg_ref[:, cols], preferred_element_type=F32)
        up = jnp.dot(z, wu_ref[:, cols], preferred_element_type=F32)
        hid_scr[:, cols] = (gate * jax.nn.sigmoid(gate) * up).astype(BF16)
    ss = jnp.zeros((x_ref.shape[0], 1), F32)
    for n0 in range(0, d, PROJ_COL_CHUNK):
        cols = slice(n0, n0 + PROJ_COL_CHUNK)
        y = x_ref[:, cols] + jnp.dot(hid_scr[...], wd_ref[:, cols], preferred_element_type=F32)
        o_ref[:, cols] = y
        ss = ss + jnp.sum(y * y, axis=-1, keepdims=True)
    o_ref[...] = o_ref[...] * lax.rsqrt(ss * (1.0 / d) + EPS) * fg_ref[...]


def _ffn(x2d, g, wg, wu, wd, fg):
    tokens, d = x2d.shape
    d_ff = wg.shape[1]
    tm = ROW_TILE
    assert tokens % tm == 0 and d_ff % FFN_COL_CHUNK == 0 and d % PROJ_COL_CHUNK == 0
    est = 4 * tm * d * 4 + 3 * d * d_ff * 2 + tm * (d + d_ff) * 2 + 6 * tm * PROJ_COL_CHUNK * 4
    const = dict(pipeline_mode=pl.Buffered(1))
    return pl.pallas_call(
        _ffn_kernel,
        grid=(tokens // tm,),
        in_specs=[
            pl.BlockSpec((tm, d), lambda i: (i, 0)),
            pl.BlockSpec((1, d), lambda i: (0, 0)),
            pl.BlockSpec((d, d_ff), lambda i: (0, 0), **const),
            pl.BlockSpec((d, d_ff), lambda i: (0, 0), **const),
            pl.BlockSpec((d_ff, d), lambda i: (0, 0), **const),
            pl.BlockSpec((1, d), lambda i: (0, 0)),
        ],
        out_specs=pl.BlockSpec((tm, d), lambda i: (i, 0)),
        out_shape=jax.ShapeDtypeStruct((tokens, d), F32),
        scratch_shapes=[pltpu.VMEM((tm, d), BF16), pltpu.VMEM((tm, d_ff), BF16)],
        compiler_params=pltpu.CompilerParams(
            dimension_semantics=("arbitrary",), vmem_limit_bytes=_vmem_limit(est)),
        name="ffn",
    )(x2d, g, wg, wu, wd, fg)


def kernel(x, norm_mix_g, w_in, lambda_q1, lambda_k1, lambda_q2, lambda_k2, subln_g, rel_bias, conv_w,
           w_out, norm_ffn_g, w_gate, w_up, w_down, final_g):
    batch, seq, d = x.shape
    assert norm_mix_g.shape[0] == 1, "single-layer trunk"
    att_width = ATT_HEADS * ATT_VH
    tokens = batch * seq
    x2d = x.reshape(tokens, d)

    proj = _in_proj(x2d, norm_mix_g[0][None, :], w_in[0].astype(BF16),
                    q_cols=att_width, q_scale=ATT_DH ** -0.5 * LOG2E)

    lamv = jnp.stack([lambda_q1[0], lambda_k1[0], lambda_q2[0], lambda_k2[0]], axis=0).astype(F32)
    att = _diff_attn(proj.reshape(batch, seq, proj.shape[1]), lamv, _bias_tiles(rel_bias, ATT_TILE),
                     subln_g[0].astype(F32)[:, None])

    x1 = _out_proj(x2d, att.reshape(tokens, att_width), proj, conv_w[0].astype(F32),
                   w_out[0].astype(BF16), seq=seq)

    out = _ffn(x1, norm_ffn_g[0][None, :], w_gate[0].astype(BF16), w_up[0].astype(BF16),
               w_down[0].astype(BF16), final_g[None, :])
    return out.reshape(batch, seq, d)
```

```python
import functools
import math

import jax
import jax.numpy as jnp
from jax import lax
from jax.experimental import pallas as pl
from jax.experimental.pallas import tpu as pltpu

F32 = jnp.float32
BF16 = jnp.bfloat16

ATT_HEADS = 4
ATT_DH = 64
ATT_VH = 2 * ATT_DH
N_BUCKETS = 32
MAX_DISTANCE = 128
EPS = 1e-6
LAMBDA_INIT = 0.8 - 0.6 * math.exp(-0.3 * 0)
LOG2E = 1.4426950408889634

LANES = 128
BF16_SUBLANES = 16
VMEM_LIMIT_CAP = 56 * 1024 * 1024

ROW_TILE = 512
ATT_TILE = 256
PROJ_COL_CHUNK = 512
FFN_COL_CHUNK = 256
M_INIT = -1e30

def _vmem_limit(nbytes):
    return int(min(max(2 * nbytes, 16 * 1024 * 1024), VMEM_LIMIT_CAP))


def _t5_bucket(rel):
    n_half = N_BUCKETS // 2
    max_exact = n_half // 2
    ret = jnp.where(rel > 0, n_half, 0).astype(jnp.int32)
    n = jnp.abs(rel).astype(jnp.int32)
    nf = jnp.maximum(n, 1).astype(jnp.float32)
    large = max_exact + (jnp.log(nf / max_exact) / math.log(MAX_DISTANCE / max_exact)
                         * (n_half - max_exact)).astype(jnp.int32)
    large = jnp.minimum(large, n_half - 1)
    return ret + jnp.where(n < max_exact, n, large)


def _bias_tiles(rel_bias, tile):
    assert tile >= MAX_DISTANCE
    kk = jnp.arange(tile, dtype=jnp.int32)[:, None]
    qq = jnp.arange(tile, dtype=jnp.int32)[None, :]
    rel = jnp.stack([dj * tile + kk - qq for dj in (-2, -1, 0, 1, 2)], axis=0)
    bucket = _t5_bucket(rel)[None]
    rb = rel_bias.astype(F32).T * LOG2E
    tiles = jnp.zeros((rb.shape[0],) + rel.shape, F32)
    for b in range(N_BUCKETS):
        tiles = jnp.where(bucket == b, rb[:, b][:, None, None, None], tiles)
    return tiles


def _in_proj_kernel(x_ref, g_ref, w_ref, o_ref, *, q_cols, q_scale):
    x = x_ref[...]
    ms = jnp.mean(x * x, axis=-1, keepdims=True)
    u = (x * lax.rsqrt(ms + EPS) * g_ref[...]).astype(BF16)
    n_out = o_ref.shape[1]
    for n0 in range(0, n_out, PROJ_COL_CHUNK):
        r = jnp.dot(u, w_ref[:, n0:n0 + PROJ_COL_CHUNK], preferred_element_type=F32)
        if n0 < q_cols:
            r = r * q_scale
        o_ref[:, n0:n0 + PROJ_COL_CHUNK] = r.astype(BF16)


def _in_proj(x2d, g, w_bf16, *, q_cols, q_scale):
    tokens, d = x2d.shape
    n_out = w_bf16.shape[1]
    assert tokens % ROW_TILE == 0 and q_cols % PROJ_COL_CHUNK == 0 and n_out % PROJ_COL_CHUNK == 0
    est = 2 * ROW_TILE * d * 4 + 2 * d * n_out * 2 + 2 * ROW_TILE * n_out * 2 + ROW_TILE * (d * 2 + PROJ_COL_CHUNK * 8)
    return pl.pallas_call(
        functools.partial(_in_proj_kernel, q_cols=q_cols, q_scale=q_scale),
        grid=(tokens // ROW_TILE,),
        in_specs=[
            pl.BlockSpec((ROW_TILE, d), lambda i: (i, 0)),
            pl.BlockSpec((1, d), lambda i: (0, 0)),
            pl.BlockSpec((d, n_out), lambda i: (0, 0)),
        ],
        out_specs=pl.BlockSpec((ROW_TILE, n_out), lambda i: (i, 0)),
        out_shape=jax.ShapeDtypeStruct((tokens, n_out), BF16),
        compiler_params=pltpu.CompilerParams(
            dimension_semantics=("arbitrary",), vmem_limit_bytes=_vmem_limit(est)),
        name="in_proj",
    )(x2d, g, w_bf16)


def _diff_attn_kernel(cfar_ref, lamv_ref, q_ref, k_ref, v_ref, bias_ref, g_ref, o_ref, vt_scr, *, n_chunks):
    tile = ATT_TILE
    i = pl.program_id(2)

    @pl.when(i == 0)
    def _():
        for j in range(n_chunks):
            vc = v_ref[0, j * tile:(j + 1) * tile, :].astype(F32)
            vt_scr[j, 0:ATT_VH, :] = vc.T.astype(BF16)
            vt_scr[j, ATT_VH:ATT_VH + BF16_SUBLANES, :] = jnp.ones((BF16_SUBLANES, tile), BF16)

    qt = q_ref[0].astype(F32).T
    row = lax.broadcasted_iota(jnp.int32, qt.shape, 0)
    qts = (jnp.where(row < ATT_DH, qt, 0.0).astype(BF16),
           jnp.where(row >= ATT_DH, qt, 0.0).astype(BF16))

    def scores(j):
        kc = k_ref[0, pl.ds(pl.multiple_of(j * tile, tile), tile), :]
        return [jnp.dot(kc, qts[c], preferred_element_type=F32) for c in range(2)]

    def update(j, s2, bias, state):
        vt = vt_scr[j]
        new_state = []
        for c in range(2):
            m_old, acc = state[2 * c], state[2 * c + 1]
            s = s2[c]
            if bias.ndim == 2:
                s = bias + s
                m_new = jnp.maximum(m_old, jnp.max(s, axis=0, keepdims=True))
                shift = m_new
            else:
                m_new = jnp.maximum(m_old, jnp.max(s, axis=0, keepdims=True) + bias)
                shift = m_new - bias
            p = jnp.exp2(s - shift).astype(BF16)
            pv = jnp.dot(vt, p, preferred_element_type=F32)
            new_state += [m_new, jnp.exp2(m_old - m_new) * acc + pv]
        return tuple(new_state)

    h = pl.program_id(1)
    c_left = cfar_ref[h, 0]
    c_right = cfar_ref[h, 1]
    m0 = jnp.full((1, tile), M_INIT, F32)
    acc0 = jnp.zeros((ATT_VH + BF16_SUBLANES, tile), F32)
    state = (m0, acc0, m0, acc0)

    order = [0, 1, n_chunks - 1] + list(range(2, n_chunks - 1))
    js = [lax.rem(i + r, n_chunks) for r in order]
    s_next = scores(js[0])
    for idx, r in enumerate(order):
        s_cur = s_next
        if idx + 1 < len(order):
            s_next = scores(js[idx + 1])
        if r in (0, 1, n_chunks - 1):
            bias = bias_ref[0, jnp.clip(js[idx] - i, -2, 2) + 2]
        else:
            bias = jnp.where(i + r < n_chunks, c_right, c_left)
        state = update(js[idx], s_cur, bias, state)

    lv = lamv_ref[...]
    lam = (jnp.exp(jnp.sum(lv[0:1] * lv[1:2], axis=-1, keepdims=True))
           - jnp.exp(jnp.sum(lv[2:3] * lv[3:4], axis=-1, keepdims=True)) + LAMBDA_INIT)
    a1 = state[1]
    a2 = state[3]
    o = a1[0:ATT_VH] / a1[ATT_VH:ATT_VH + 1] - lam * (a2[0:ATT_VH] / a2[ATT_VH:ATT_VH + 1])
    ms = jnp.mean(o * o, axis=0, keepdims=True)
    y = o * lax.rsqrt(ms + EPS) * g_ref[...] * (1.0 - LAMBDA_INIT)
    o_ref[0] = y.T.astype(BF16)


def _diff_attn(proj3d, lamv, bias_tiles, subln_col):
    batch, seq, _ = proj3d.shape
    tile = ATT_TILE
    assert seq % tile == 0 and 2 * ATT_DH == LANES and ATT_VH == LANES
    n_chunks = seq // tile
    acc_rows = ATT_VH + BF16_SUBLANES
    k_blk = ATT_HEADS
    v_blk = 2 * ATT_HEADS
    est = (4 * seq * LANES * 2 + 2 * 5 * tile * tile * 4 + n_chunks * acc_rows * tile * 2
           + 2 * acc_rows * tile * 4 + 2 * n_chunks * tile * tile * 4)
    cfar = jnp.stack([bias_tiles[:, 0, 0, 0], bias_tiles[:, 4, 0, 0]], axis=1)
    return pl.pallas_call(
        functools.partial(_diff_attn_kernel, n_chunks=n_chunks),
        grid=(batch, ATT_HEADS, n_chunks),
        in_specs=[
            pl.BlockSpec(memory_space=pltpu.SMEM),
            pl.BlockSpec((4, ATT_DH), lambda b, h, i: (0, 0)),
            pl.BlockSpec((1, tile, LANES), lambda b, h, i: (b, i, h)),
            pl.BlockSpec((1, seq, LANES), lambda b, h, i: (b, 0, k_blk + h)),
            pl.BlockSpec((1, seq, LANES), lambda b, h, i: (b, 0, v_blk + h)),
            pl.BlockSpec((1, 5, tile, tile), lambda b, h, i: (h, 0, 0, 0)),
            pl.BlockSpec((ATT_VH, 1), lambda b, h, i: (0, 0)),
        ],
        out_specs=pl.BlockSpec((1, tile, LANES), lambda b, h, i: (b, i, h)),
        out_shape=jax.ShapeDtypeStruct((batch, seq, ATT_HEADS * ATT_VH), BF16),
        scratch_shapes=[pltpu.VMEM((n_chunks, acc_rows, tile), BF16)],
        compiler_params=pltpu.CompilerParams(
            dimension_semantics=("arbitrary", "arbitrary", "arbitrary"),
            vmem_limit_bytes=_vmem_limit(est)),
        name="diff_attn",
    )(cfar, lamv, proj3d, proj3d, proj3d, bias_tiles, subln_col)


def _out_proj_kernel(x_ref, att_ref, b_ref, c_ref, h_ref, cp_ref, hp_ref, cn_ref, hn_ref, cw_ref, w_ref,
                     o_ref, gpad_scr, *, tiles_per_seq):
    tm = x_ref.shape[0]
    att_w = att_ref.shape[1]
    i = pl.program_id(0)
    pos = i % tiles_per_seq
    g = c_ref[...].astype(F32) * h_ref[...].astype(F32)
    g_prev = (cp_ref[...].astype(F32) * hp_ref[...].astype(F32))[BF16_SUBLANES - 1:BF16_SUBLANES]
    g_next = (cn_ref[...].astype(F32) * hn_ref[...].astype(F32))[0:1]
    g_prev = jnp.where(pos == 0, 0.0, g_prev)
    g_next = jnp.where(pos == tiles_per_seq - 1, 0.0, g_next)
    gpad_scr[7:8, :] = g_prev
    gpad_scr[8:8 + tm, :] = g
    gpad_scr[8 + tm:9 + tm, :] = g_next
    cw = cw_ref[...]
    conv = b_ref[...].astype(F32) * (cw[0:1] * gpad_scr[7:7 + tm, :] + cw[1:2] * g
                                     + cw[2:3] * gpad_scr[9:9 + tm, :])
    conv = conv.astype(BF16)
    att = att_ref[...]
    for n0 in range(0, o_ref.shape[1], PROJ_COL_CHUNK):
        cols = slice(n0, n0 + PROJ_COL_CHUNK)
        y = jnp.dot(att, w_ref[0:att_w, cols], preferred_element_type=F32)
        y = y + jnp.dot(conv, w_ref[att_w:, cols], preferred_element_type=F32)
        o_ref[:, cols] = x_ref[:, cols] + y


def _out_proj(x2d, att2d, proj2d, conv_w, w_bf16, *, seq):
    tokens, d = x2d.shape
    att_w = att2d.shape[1]
    conv_width = conv_w.shape[1]
    tm = ROW_TILE
    assert seq % tm == 0 and att_w == conv_width and d % PROJ_COL_CHUNK == 0
    tiles_per_seq = seq // tm
    blk0 = (3 * att_w) // conv_width
    halo = BF16_SUBLANES
    rows_per_tile = tm // halo
    last_halo = tokens // halo - 1
    est = (4 * tm * d * 4 + 8 * tm * conv_width * 2 + 2 * (att_w + conv_width) * d * 2
           + (tm + 16) * conv_width * 4 + 6 * tm * conv_width * 4)
    return pl.pallas_call(
        functools.partial(_out_proj_kernel, tiles_per_seq=tiles_per_seq),
        grid=(tokens // tm,),
        in_specs=[
            pl.BlockSpec((tm, d), lambda i: (i, 0)),
            pl.BlockSpec((tm, att_w), lambda i: (i, 0)),
            pl.BlockSpec((tm, conv_width), lambda i: (i, blk0)),
            pl.BlockSpec((tm, conv_width), lambda i: (i, blk0 + 1)),
            pl.BlockSpec((tm, conv_width), lambda i: (i, blk0 + 2)),
            pl.BlockSpec((halo, conv_width), lambda i: (jnp.maximum(i * rows_per_tile - 1, 0), blk0 + 1)),
            pl.BlockSpec((halo, conv_width), lambda i: (jnp.maximum(i * rows_per_tile - 1, 0), blk0 + 2)),
            pl.BlockSpec((halo, conv_width),
                         lambda i: (jnp.minimum((i + 1) * rows_per_tile, last_halo), blk0 + 1)),
            pl.BlockSpec((halo, conv_width),
                         lambda i: (jnp.minimum((i + 1) * rows_per_tile, last_halo), blk0 + 2)),
            pl.BlockSpec(conv_w.shape, lambda i: (0, 0)),
            pl.BlockSpec(w_bf16.shape, lambda i: (0, 0)),
        ],
        out_specs=pl.BlockSpec((tm, d), lambda i: (i, 0)),
        out_shape=jax.ShapeDtypeStruct((tokens, d), F32),
        scratch_shapes=[pltpu.VMEM((tm + 16, conv_width), F32)],
        compiler_params=pltpu.CompilerParams(
            dimension_semantics=("arbitrary",), vmem_limit_bytes=_vmem_limit(est)),
        name="out_proj",
    )(x2d, att2d, proj2d, proj2d, proj2d, proj2d, proj2d, proj2d, proj2d, conv_w, w_bf16)


def _ffn_kernel(x_ref, g_ref, wg_ref, wu_ref, wd_ref, fg_ref, o_ref, z_scr, hid_scr):
    d = x_ref.shape[1]
    d_ff = wg_ref.shape[1]
    x = x_ref[...]
    ms = jnp.mean(x * x, axis=-1, keepdims=True)
    z_scr[...] = (x * lax.rsqrt(ms + EPS) * g_ref[...]).astype(BF16)
    for f0 in range(0, d_ff, FFN_COL_CHUNK):
        cols = slice(f0, f0 + FFN_COL_CHUNK)
        z = z_scr[...]
        gate = jnp.dot(z, wg_ref[:, cols], preferred_element_type=F32)
        up = jnp.dot(z, wu_ref[:, cols], preferred_element_type=F32)
        hid_scr[:, cols] = (gate * jax.nn.sigmoid(gate) * up).astype(BF16)
    ss = jnp.zeros((x_ref.shape[0], 1), F32)
    for n0 in range(0, d, PROJ_COL_CHUNK):
        cols = slice(n0, n0 + PROJ_COL_CHUNK)
        y = x_ref[:, cols] + jnp.dot(hid_scr[...], wd_ref[:, cols], preferred_element_type=F32)
        o_ref[:, cols] = y
        ss = ss + jnp.sum(y * y, axis=-1, keepdims=True)
    o_ref[...] = o_ref[...] * lax.rsqrt(ss * (1.0 / d) + EPS) * fg_ref[...]


def _ffn(x2d, g, wg, wu, wd, fg):
    tokens, d = x2d.shape
    d_ff = wg.shape[1]
    tm = ROW_TILE
    assert tokens % tm == 0 and d_ff % FFN_COL_CHUNK == 0 and d % PROJ_COL_CHUNK == 0
    est = 4 * tm * d * 4 + 3 * d * d_ff * 2 + tm * (d + d_ff) * 2 + 6 * tm * PROJ_COL_CHUNK * 4
    const = dict(pipeline_mode=pl.Buffered(1))
    return pl.pallas_call(
        _ffn_kernel,
        grid=(tokens // tm,),
        in_specs=[
            pl.BlockSpec((tm, d), lambda i: (i, 0)),
            pl.BlockSpec((1, d), lambda i: (0, 0)),
            pl.BlockSpec((d, d_ff), lambda i: (0, 0), **const),
            pl.BlockSpec((d, d_ff), lambda i: (0, 0), **const),
            pl.BlockSpec((d_ff, d), lambda i: (0, 0), **const),
            pl.BlockSpec((1, d), lambda i: (0, 0)),
        ],
        out_specs=pl.BlockSpec((tm, d), lambda i: (i, 0)),
        out_shape=jax.ShapeDtypeStruct((tokens, d), F32),
        scratch_shapes=[pltpu.VMEM((tm, d), BF16), pltpu.VMEM((tm, d_ff), BF16)],
        compiler_params=pltpu.CompilerParams(
            dimension_semantics=("arbitrary",), vmem_limit_bytes=_vmem_limit(est)),
        name="ffn",
    )(x2d, g, wg, wu, wd, fg)


def kernel(x, norm_mix_g, w_in, lambda_q1, lambda_k1, lambda_q2, lambda_k2, subln_g, rel_bias, conv_w,
           w_out, norm_ffn_g, w_gate, w_up, w_down, final_g):
    batch, seq, d = x.shape
    assert norm_mix_g.shape[0] == 1, "single-layer trunk"
    att_width = ATT_HEADS * ATT_VH
    tokens = batch * seq
    x2d = x.reshape(tokens, d)

    proj = _in_proj(x2d, norm_mix_g[0][None, :], w_in[0].astype(BF16),
                    q_cols=att_width, q_scale=ATT_DH ** -0.5 * LOG2E)

    lamv = jnp.stack([lambda_q1[0], lambda_k1[0], lambda_q2[0], lambda_k2[0]], axis=0).astype(F32)
    att = _diff_attn(proj.reshape(batch, seq, proj.shape[1]), lamv, _bias_tiles(rel_bias, ATT_TILE),
                     subln_g[0].astype(F32)[:, None])

    x1 = _out_proj(x2d, att.reshape(tokens, att_width), proj, conv_w[0].astype(F32),
                   w_out[0].astype(BF16), seq=seq)

    out = _ffn(x1, norm_ffn_g[0][None, :], w_gate[0].astype(BF16), w_up[0].astype(BF16),
               w_down[0].astype(BF16), final_g[None, :])
    return out.reshape(batch, seq, d)
```

```python
import functools
import math

import jax
import jax.numpy as jnp
from jax import lax
from jax.experimental import pallas as pl
from jax.experimental.pallas import tpu as pltpu

F32 = jnp.float32
BF16 = jnp.bfloat16

ATT_HEADS = 4
ATT_DH = 64
ATT_VH = 2 * ATT_DH
N_BUCKETS = 32
MAX_DISTANCE = 128
EPS = 1e-6
LAMBDA_INIT = 0.8 - 0.6 * math.exp(-0.3 * 0)
LOG2E = 1.4426950408889634

LANES = 128
BF16_SUBLANES = 16
VMEM_LIMIT_CAP = 56 * 1024 * 1024

ROW_TILE = 512
ATT_TILE = 256
PROJ_COL_CHUNK = 512
FFN_COL_CHUNK = 256
M_INIT = -1e30
SCORE_LOOKAHEAD = 2
TILES_PER_BODY = 4

def _vmem_limit(nbytes):
    return int(min(max(2 * nbytes, 16 * 1024 * 1024), VMEM_LIMIT_CAP))


def _t5_bucket(rel):
    n_half = N_BUCKETS // 2
    max_exact = n_half // 2
    ret = jnp.where(rel > 0, n_half, 0).astype(jnp.int32)
    n = jnp.abs(rel).astype(jnp.int32)
    nf = jnp.maximum(n, 1).astype(jnp.float32)
    large = max_exact + (jnp.log(nf / max_exact) / math.log(MAX_DISTANCE / max_exact)
                         * (n_half - max_exact)).astype(jnp.int32)
    large = jnp.minimum(large, n_half - 1)
    return ret + jnp.where(n < max_exact, n, large)


def _bias_tiles(rel_bias, tile):
    assert tile >= MAX_DISTANCE
    kk = jnp.arange(tile, dtype=jnp.int32)[:, None]
    qq = jnp.arange(tile, dtype=jnp.int32)[None, :]
    rel = jnp.stack([dj * tile + kk - qq for dj in (-2, -1, 0, 1, 2)], axis=0)
    bucket = _t5_bucket(rel)[None]
    rb = rel_bias.astype(F32).T * LOG2E
    tiles = jnp.zeros((rb.shape[0],) + rel.shape, F32)
    for b in range(N_BUCKETS):
        tiles = jnp.where(bucket == b, rb[:, b][:, None, None, None], tiles)
    return tiles


def _in_proj_kernel(x_ref, g_ref, w_ref, o_ref, *, q_cols, q_scale):
    x = x_ref[...]
    ms = jnp.mean(x * x, axis=-1, keepdims=True)
    u = (x * lax.rsqrt(ms + EPS) * g_ref[...]).astype(BF16)
    n_out = o_ref.shape[1]
    for n0 in range(0, n_out, PROJ_COL_CHUNK):
        r = jnp.dot(u, w_ref[:, n0:n0 + PROJ_COL_CHUNK], preferred_element_type=F32)
        if n0 < q_cols:
            r = r * q_scale
        o_ref[:, n0:n0 + PROJ_COL_CHUNK] = r.astype(BF16)


def _in_proj(x2d, g, w_bf16, *, q_cols, q_scale):
    tokens, d = x2d.shape
    n_out = w_bf16.shape[1]
    assert tokens % ROW_TILE == 0 and q_cols % PROJ_COL_CHUNK == 0 and n_out % PROJ_COL_CHUNK == 0
    est = 2 * ROW_TILE * d * 4 + 2 * d * n_out * 2 + 2 * ROW_TILE * n_out * 2 + ROW_TILE * (d * 2 + PROJ_COL_CHUNK * 8)
    return pl.pallas_call(
        functools.partial(_in_proj_kernel, q_cols=q_cols, q_scale=q_scale),
        grid=(tokens // ROW_TILE,),
        in_specs=[
            pl.BlockSpec((ROW_TILE, d), lambda i: (i, 0)),
            pl.BlockSpec((1, d), lambda i: (0, 0)),
            pl.BlockSpec((d, n_out), lambda i: (0, 0)),
        ],
        out_specs=pl.BlockSpec((ROW_TILE, n_out), lambda i: (i, 0)),
        out_shape=jax.ShapeDtypeStruct((tokens, n_out), BF16),
        compiler_params=pltpu.CompilerParams(
            dimension_semantics=("arbitrary",), vmem_limit_bytes=_vmem_limit(est)),
        name="in_proj",
    )(x2d, g, w_bf16)


def _diff_attn_kernel(cfar_ref, lamv_ref, q_ref, k_ref, v_ref, bias_ref, g_ref, o_ref, vt_scr, *, n_chunks):
    tile = ATT_TILE
    h = pl.program_id(1)

    for j in range(n_chunks):
        vc = v_ref[0, j * tile:(j + 1) * tile, :].astype(F32)
        vt_scr[j, 0:ATT_VH, :] = vc.T.astype(BF16)
        vt_scr[j, ATT_VH:ATT_VH + BF16_SUBLANES, :] = jnp.ones((BF16_SUBLANES, tile), BF16)

    lv = lamv_ref[...]
    lam = (jnp.exp(jnp.sum(lv[0:1] * lv[1:2], axis=-1, keepdims=True))
           - jnp.exp(jnp.sum(lv[2:3] * lv[3:4], axis=-1, keepdims=True)) + LAMBDA_INIT)
    c_left = cfar_ref[h, 0]
    c_right = cfar_ref[h, 1]

    def update(j, s2, bias, state):
        vt = vt_scr[j]
        new_state = []
        for c in range(2):
            m_old, acc = state[2 * c], state[2 * c + 1]
            s = s2[c]
            if bias.ndim == 2:
                s = bias + s
                m_new = jnp.maximum(m_old, jnp.max(s, axis=0, keepdims=True))
                shift = m_new
            else:
                m_new = jnp.maximum(m_old, jnp.max(s, axis=0, keepdims=True) + bias)
                shift = m_new - bias
            p = jnp.exp2(s - shift).astype(BF16)
            pv = jnp.dot(vt, p, preferred_element_type=F32)
            new_state += [m_new, jnp.exp2(m_old - m_new) * acc + pv]
        return tuple(new_state)

    order = [0, 1, n_chunks - 1] + list(range(2, n_chunks - 1))
    m0 = jnp.full((1, tile), M_INIT, F32)
    acc0 = jnp.zeros((ATT_VH + BF16_SUBLANES, tile), F32)

    def query_tiles(it, carry):
        steps = [(t, idx) for t in range(TILES_PER_BODY) for idx in range(n_chunks)]
        ctx = {}

        def tile_ctx(t):
            if t not in ctx:
                i = it * TILES_PER_BODY + t
                rows = pl.ds(pl.multiple_of(i * tile, tile), tile)
                qt = q_ref[0, rows, :].astype(F32).T
                row = lax.broadcasted_iota(jnp.int32, qt.shape, 0)
                qts = (jnp.where(row < ATT_DH, qt, 0.0).astype(BF16),
                       jnp.where(row >= ATT_DH, qt, 0.0).astype(BF16))
                ctx[t] = (i, rows, qts, [lax.rem(i + r, n_chunks) for r in order])
            return ctx[t]

        def scores(step):
            _, _, qts, js = tile_ctx(step[0])
            kc = k_ref[0, pl.ds(pl.multiple_of(js[step[1]] * tile, tile), tile), :]
            return [jnp.dot(kc, qts[c], preferred_element_type=F32) for c in range(2)]

        pending = [scores(steps[d]) for d in range(SCORE_LOOKAHEAD)]
        state = None
        for n, (t, idx) in enumerate(steps):
            i, rows, _, js = tile_ctx(t)
            r = order[idx]
            if idx == 0:
                state = (m0, acc0, m0, acc0)
            s_cur = pending.pop(0)
            if n + SCORE_LOOKAHEAD < len(steps):
                pending.append(scores(steps[n + SCORE_LOOKAHEAD]))
            if r in (0, 1, n_chunks - 1):
                bias = bias_ref[0, jnp.clip(js[idx] - i, -2, 2) + 2]
            else:
                bias = jnp.where(i + r < n_chunks, c_right, c_left)
            state = update(js[idx], s_cur, bias, state)
            if idx == n_chunks - 1:
                a1 = state[1]
                a2 = state[3]
                o = a1[0:ATT_VH] / a1[ATT_VH:ATT_VH + 1] - lam * (a2[0:ATT_VH] / a2[ATT_VH:ATT_VH + 1])
                ms = jnp.mean(o * o, axis=0, keepdims=True)
                y = o * lax.rsqrt(ms + EPS) * g_ref[...] * (1.0 - LAMBDA_INIT)
                o_ref[0, rows, :] = y.T.astype(BF16)
        return carry

    lax.fori_loop(0, n_chunks // TILES_PER_BODY, query_tiles, 0)


def _diff_attn(proj3d, lamv, bias_tiles, subln_col):
    batch, seq, _ = proj3d.shape
    tile = ATT_TILE
    assert seq % tile == 0 and 2 * ATT_DH == LANES and ATT_VH == LANES
    n_chunks = seq // tile
    acc_rows = ATT_VH + BF16_SUBLANES
    k_blk = ATT_HEADS
    v_blk = 2 * ATT_HEADS
    est = (8 * seq * LANES * 2 + 2 * 5 * tile * tile * 4 + n_chunks * acc_rows * tile * 2
           + 2 * acc_rows * tile * 4 + 2 * n_chunks * tile * tile * 4)
    cfar = jnp.stack([bias_tiles[:, 0, 0, 0], bias_tiles[:, 4, 0, 0]], axis=1)
    return pl.pallas_call(
        functools.partial(_diff_attn_kernel, n_chunks=n_chunks),
        grid=(batch, ATT_HEADS),
        in_specs=[
            pl.BlockSpec(memory_space=pltpu.SMEM),
            pl.BlockSpec((4, ATT_DH), lambda b, h: (0, 0)),
            pl.BlockSpec((1, seq, LANES), lambda b, h: (b, 0, h)),
            pl.BlockSpec((1, seq, LANES), lambda b, h: (b, 0, k_blk + h)),
            pl.BlockSpec((1, seq, LANES), lambda b, h: (b, 0, v_blk + h)),
            pl.BlockSpec((1, 5, tile, tile), lambda b, h: (h, 0, 0, 0)),
            pl.BlockSpec((ATT_VH, 1), lambda b, h: (0, 0)),
        ],
        out_specs=pl.BlockSpec((1, seq, LANES), lambda b, h: (b, 0, h)),
        out_shape=jax.ShapeDtypeStruct((batch, seq, ATT_HEADS * ATT_VH), BF16),
        scratch_shapes=[pltpu.VMEM((n_chunks, acc_rows, tile), BF16)],
        compiler_params=pltpu.CompilerParams(
            dimension_semantics=("arbitrary", "arbitrary"),
            vmem_limit_bytes=_vmem_limit(est)),
        name="diff_attn",
    )(cfar, lamv, proj3d, proj3d, proj3d, bias_tiles, subln_col)


def _out_proj_kernel(x_ref, att_ref, b_ref, c_ref, h_ref, cp_ref, hp_ref, cn_ref, hn_ref, cw_ref, w_ref,
                     o_ref, gpad_scr, *, tiles_per_seq):
    tm = x_ref.shape[0]
    att_w = att_ref.shape[1]
    i = pl.program_id(0)
    pos = i % tiles_per_seq
    g = c_ref[...].astype(F32) * h_ref[...].astype(F32)
    g_prev = (cp_ref[...].astype(F32) * hp_ref[...].astype(F32))[BF16_SUBLANES - 1:BF16_SUBLANES]
    g_next = (cn_ref[...].astype(F32) * hn_ref[...].astype(F32))[0:1]
    g_prev = jnp.where(pos == 0, 0.0, g_prev)
    g_next = jnp.where(pos == tiles_per_seq - 1, 0.0, g_next)
    gpad_scr[7:8, :] = g_prev
    gpad_scr[8:8 + tm, :] = g
    gpad_scr[8 + tm:9 + tm, :] = g_next
    cw = cw_ref[...]
    conv = b_ref[...].astype(F32) * (cw[0:1] * gpad_scr[7:7 + tm, :] + cw[1:2] * g
                                     + cw[2:3] * gpad_scr[9:9 + tm, :])
    conv = conv.astype(BF16)
    att = att_ref[...]
    for n0 in range(0, o_ref.shape[1], PROJ_COL_CHUNK):
        cols = slice(n0, n0 + PROJ_COL_CHUNK)
        y = jnp.dot(att, w_ref[0:att_w, cols], preferred_element_type=F32)
        y = y + jnp.dot(conv, w_ref[att_w:, cols], preferred_element_type=F32)
        o_ref[:, cols] = x_ref[:, cols] + y


def _out_proj(x2d, att2d, proj2d, conv_w, w_bf16, *, seq):
    tokens, d = x2d.shape
    att_w = att2d.shape[1]
    conv_width = conv_w.shape[1]
    tm = ROW_TILE
    assert seq % tm == 0 and att_w == conv_width and d % PROJ_COL_CHUNK == 0
    tiles_per_seq = seq // tm
    blk0 = (3 * att_w) // conv_width
    halo = BF16_SUBLANES
    rows_per_tile = tm // halo
    last_halo = tokens // halo - 1
    est = (4 * tm * d * 4 + 8 * tm * conv_width * 2 + 2 * (att_w + conv_width) * d * 2
           + (tm + 16) * conv_width * 4 + 6 * tm * conv_width * 4)
    return pl.pallas_call(
        functools.partial(_out_proj_kernel, tiles_per_seq=tiles_per_seq),
        grid=(tokens // tm,),
        in_specs=[
            pl.BlockSpec((tm, d), lambda i: (i, 0)),
            pl.BlockSpec((tm, att_w), lambda i: (i, 0)),
            pl.BlockSpec((tm, conv_width), lambda i: (i, blk0)),
            pl.BlockSpec((tm, conv_width), lambda i: (i, blk0 + 1)),
            pl.BlockSpec((tm, conv_width), lambda i: (i, blk0 + 2)),
            pl.BlockSpec((halo, conv_width), lambda i: (jnp.maximum(i * rows_per_tile - 1, 0), blk0 + 1)),
            pl.BlockSpec((halo, conv_width), lambda i: (jnp.maximum(i * rows_per_tile - 1, 0), blk0 + 2)),
            pl.BlockSpec((halo, conv_width),
                         lambda i: (jnp.minimum((i + 1) * rows_per_tile, last_halo), blk0 + 1)),
            pl.BlockSpec((halo, conv_width),
                         lambda i: (jnp.minimum((i + 1) * rows_per_tile, last_halo), blk0 + 2)),
            pl.BlockSpec(conv_w.shape, lambda i: (0, 0)),
            pl.BlockSpec(w_bf16.shape, lambda i: (0, 0)),
        ],
        out_specs=pl.BlockSpec((tm, d), lambda i: (i, 0)),
        out_shape=jax.ShapeDtypeStruct((tokens, d), F32),
        scratch_shapes=[pltpu.VMEM((tm + 16, conv_width), F32)],
        compiler_params=pltpu.CompilerParams(
            dimension_semantics=("arbitrary",), vmem_limit_bytes=_vmem_limit(est)),
        name="out_proj",
    )(x2d, att2d, proj2d, proj2d, proj2d, proj2d, proj2d, proj2d, proj2d, conv_w, w_bf16)


def _ffn_kernel(x_ref, g_ref, wg_ref, wu_ref, wd_ref, fg_ref, o_ref, z_scr, hid_scr):
    d = x_ref.shape[1]
    d_ff = wg_ref.shape[1]
    x = x_ref[...]
    ms = jnp.mean(x * x, axis=-1, keepdims=True)
    z_scr[...] = (x * lax.rsqrt(ms + EPS) * g_ref[...]).astype(BF16)
    for f0 in range(0, d_ff, FFN_COL_CHUNK):
        cols = slice(f0, f0 + FFN_COL_CHUNK)
        z = z_scr[...]
        gate = jnp.dot(z, wg_ref[:, cols], preferred_element_type=F32)
        up = jnp.dot(z, wu_ref[:, cols], preferred_element_type=F32)
        hid_scr[:, cols] = (gate * jax.nn.sigmoid(gate) * up).astype(BF16)
    ss = jnp.zeros((x_ref.shape[0], 1), F32)
    for n0 in range(0, d, PROJ_COL_CHUNK):
        cols = slice(n0, n0 + PROJ_COL_CHUNK)
        y = x_ref[:, cols] + jnp.dot(hid_scr[...], wd_ref[:, cols], preferred_element_type=F32)
        o_ref[:, cols] = y
        ss = ss + jnp.sum(y * y, axis=-1, keepdims=True)
    o_ref[...] = o_ref[...] * lax.rsqrt(ss * (1.0 / d) + EPS) * fg_ref[...]


def _ffn(x2d, g, wg, wu, wd, fg):
    tokens, d = x2d.shape
    d_ff = wg.shape[1]
    tm = ROW_TILE
    assert tokens % tm == 0 and d_ff % FFN_COL_CHUNK == 0 and d % PROJ_COL_CHUNK == 0
    est = 4 * tm * d * 4 + 3 * d * d_ff * 2 + tm * (d + d_ff) * 2 + 6 * tm * PROJ_COL_CHUNK * 4
    const = dict(pipeline_mode=pl.Buffered(1))
    return pl.pallas_call(
        _ffn_kernel,
        grid=(tokens // tm,),
        in_specs=[
            pl.BlockSpec((tm, d), lambda i: (i, 0)),
            pl.BlockSpec((1, d), lambda i: (0, 0)),
            pl.BlockSpec((d, d_ff), lambda i: (0, 0), **const),
            pl.BlockSpec((d, d_ff), lambda i: (0, 0), **const),
            pl.BlockSpec((d_ff, d), lambda i: (0, 0), **const),
            pl.BlockSpec((1, d), lambda i: (0, 0)),
        ],
        out_specs=pl.BlockSpec((tm, d), lambda i: (i, 0)),
        out_shape=jax.ShapeDtypeStruct((tokens, d), F32),
        scratch_shapes=[pltpu.VMEM((tm, d), BF16), pltpu.VMEM((tm, d_ff), BF16)],
        compiler_params=pltpu.CompilerParams(
            dimension_semantics=("arbitrary",), vmem_limit_bytes=_vmem_limit(est)),
        name="ffn",
    )(x2d, g, wg, wu, wd, fg)


def kernel(x, norm_mix_g, w_in, lambda_q1, lambda_k1, lambda_q2, lambda_k2, subln_g, rel_bias, conv_w,
           w_out, norm_ffn_g, w_gate, w_up, w_down, final_g):
    batch, seq, d = x.shape
    assert norm_mix_g.shape[0] == 1, "single-layer trunk"
    att_width = ATT_HEADS * ATT_VH
    tokens = batch * seq
    x2d = x.reshape(tokens, d)

    proj = _in_proj(x2d, norm_mix_g[0][None, :], w_in[0].astype(BF16),
                    q_cols=att_width, q_scale=ATT_DH ** -0.5 * LOG2E)

    lamv = jnp.stack([lambda_q1[0], lambda_k1[0], lambda_q2[0], lambda_k2[0]], axis=0).astype(F32)
    att = _diff_attn(proj.reshape(batch, seq, proj.shape[1]), lamv, _bias_tiles(rel_bias, ATT_TILE),
                     subln_g[0].astype(F32)[:, None])

    x1 = _out_proj(x2d, att.reshape(tokens, att_width), proj, conv_w[0].astype(F32),
                   w_out[0].astype(BF16), seq=seq)

    out = _ffn(x1, norm_ffn_g[0][None, :], w_gate[0].astype(BF16), w_up[0].astype(BF16),
               w_down[0].astype(BF16), final_g[None, :])
    return out.reshape(batch, seq, d)
```

```python
import functools
import math

import jax
import jax.numpy as jnp
from jax import lax
from jax.experimental import pallas as pl
from jax.experimental.pallas import tpu as pltpu

F32 = jnp.float32
BF16 = jnp.bfloat16

ATT_HEADS = 4
ATT_DH = 64
ATT_VH = 2 * ATT_DH
N_BUCKETS = 32
MAX_DISTANCE = 128
EPS = 1e-6
LAMBDA_INIT = 0.8 - 0.6 * math.exp(-0.3 * 0)
LOG2E = 1.4426950408889634

LANES = 128
BF16_SUBLANES = 16
VMEM_LIMIT_CAP = 56 * 1024 * 1024

ROW_TILE = 512
ATT_TILE = 256
PROJ_COL_CHUNK = 512
FFN_COL_CHUNK = 256
M_INIT = -1e30
SCORE_LOOKAHEAD = 3
TILES_PER_BODY = 4

def _vmem_limit(nbytes):
    return int(min(max(2 * nbytes, 16 * 1024 * 1024), VMEM_LIMIT_CAP))


def _t5_bucket(rel):
    n_half = N_BUCKETS // 2
    max_exact = n_half // 2
    ret = jnp.where(rel > 0, n_half, 0).astype(jnp.int32)
    n = jnp.abs(rel).astype(jnp.int32)
    nf = jnp.maximum(n, 1).astype(jnp.float32)
    large = max_exact + (jnp.log(nf / max_exact) / math.log(MAX_DISTANCE / max_exact)
                         * (n_half - max_exact)).astype(jnp.int32)
    large = jnp.minimum(large, n_half - 1)
    return ret + jnp.where(n < max_exact, n, large)


def _bias_tiles(rel_bias, tile):
    assert tile >= MAX_DISTANCE
    kk = jnp.arange(tile, dtype=jnp.int32)[:, None]
    qq = jnp.arange(tile, dtype=jnp.int32)[None, :]
    rel = jnp.stack([dj * tile + kk - qq for dj in (-2, -1, 0, 1, 2)], axis=0)
    bucket = _t5_bucket(rel)[None]
    rb = rel_bias.astype(F32).T * LOG2E
    tiles = jnp.zeros((rb.shape[0],) + rel.shape, F32)
    for b in range(N_BUCKETS):
        tiles = jnp.where(bucket == b, rb[:, b][:, None, None, None], tiles)
    return tiles


def _in_proj_kernel(x_ref, g_ref, w_ref, o_ref, *, q_cols, q_scale):
    x = x_ref[...]
    ms = jnp.mean(x * x, axis=-1, keepdims=True)
    u = (x * lax.rsqrt(ms + EPS) * g_ref[...]).astype(BF16)
    n_out = o_ref.shape[1]
    for n0 in range(0, n_out, PROJ_COL_CHUNK):
        r = jnp.dot(u, w_ref[:, n0:n0 + PROJ_COL_CHUNK], preferred_element_type=F32)
        if n0 < q_cols:
            r = r * q_scale
        o_ref[:, n0:n0 + PROJ_COL_CHUNK] = r.astype(BF16)


def _in_proj(x2d, g, w_bf16, *, q_cols, q_scale):
    tokens, d = x2d.shape
    n_out = w_bf16.shape[1]
    assert tokens % ROW_TILE == 0 and q_cols % PROJ_COL_CHUNK == 0 and n_out % PROJ_COL_CHUNK == 0
    est = 2 * ROW_TILE * d * 4 + 2 * d * n_out * 2 + 2 * ROW_TILE * n_out * 2 + ROW_TILE * (d * 2 + PROJ_COL_CHUNK * 8)
    return pl.pallas_call(
        functools.partial(_in_proj_kernel, q_cols=q_cols, q_scale=q_scale),
        grid=(tokens // ROW_TILE,),
        in_specs=[
            pl.BlockSpec((ROW_TILE, d), lambda i: (i, 0)),
            pl.BlockSpec((1, d), lambda i: (0, 0)),
            pl.BlockSpec((d, n_out), lambda i: (0, 0)),
        ],
        out_specs=pl.BlockSpec((ROW_TILE, n_out), lambda i: (i, 0)),
        out_shape=jax.ShapeDtypeStruct((tokens, n_out), BF16),
        compiler_params=pltpu.CompilerParams(
            dimension_semantics=("arbitrary",), vmem_limit_bytes=_vmem_limit(est)),
        name="in_proj",
    )(x2d, g, w_bf16)


def _diff_attn_kernel(cfar_ref, lamv_ref, q_ref, k_ref, v_ref, bias_ref, g_ref, o_ref, vt_scr, *, n_chunks):
    tile = ATT_TILE
    h = pl.program_id(1)

    for j in range(n_chunks):
        vc = v_ref[0, j * tile:(j + 1) * tile, :].astype(F32)
        vt_scr[j, 0:ATT_VH, :] = vc.T.astype(BF16)
        vt_scr[j, ATT_VH:ATT_VH + BF16_SUBLANES, :] = jnp.ones((BF16_SUBLANES, tile), BF16)

    lv = lamv_ref[...]
    lam = (jnp.exp(jnp.sum(lv[0:1] * lv[1:2], axis=-1, keepdims=True))
           - jnp.exp(jnp.sum(lv[2:3] * lv[3:4], axis=-1, keepdims=True)) + LAMBDA_INIT)
    c_left = cfar_ref[h, 0]
    c_right = cfar_ref[h, 1]

    def update(j, s2, bias, state):
        vt = vt_scr[j]
        new_state = []
        for c in range(2):
            m_old, acc = state[2 * c], state[2 * c + 1]
            s = s2[c]
            if bias.ndim == 2:
                s = (bias + s).astype(BF16)
                offset = 0.0
            else:
                s = s.astype(BF16)
                offset = bias
            m_chunk = jnp.max(s, axis=0, keepdims=True).astype(F32) + offset
            shift = (jnp.maximum(m_old, m_chunk) - offset).astype(BF16)
            m_new = shift.astype(F32) + offset
            p = jnp.exp2(s - shift)
            pv = jnp.dot(vt, p, preferred_element_type=F32)
            new_state += [m_new, jnp.exp2(m_old - m_new) * acc + pv]
        return tuple(new_state)

    order = [0, 1, n_chunks - 1] + list(range(2, n_chunks - 1))
    m0 = jnp.full((1, tile), M_INIT, F32)
    acc0 = jnp.zeros((ATT_VH + BF16_SUBLANES, tile), F32)

    def query_tiles(it, carry):
        steps = [(t, idx) for t in range(TILES_PER_BODY) for idx in range(n_chunks)]
        ctx = {}

        def tile_ctx(t):
            if t not in ctx:
                i = it * TILES_PER_BODY + t
                rows = pl.ds(pl.multiple_of(i * tile, tile), tile)
                qt = q_ref[0, rows, :].astype(F32).T
                row = lax.broadcasted_iota(jnp.int32, qt.shape, 0)
                qts = (jnp.where(row < ATT_DH, qt, 0.0).astype(BF16),
                       jnp.where(row >= ATT_DH, qt, 0.0).astype(BF16))
                ctx[t] = (i, rows, qts, [lax.rem(i + r, n_chunks) for r in order])
            return ctx[t]

        def scores(step):
            _, _, qts, js = tile_ctx(step[0])
            kc = k_ref[0, pl.ds(pl.multiple_of(js[step[1]] * tile, tile), tile), :]
            return [jnp.dot(kc, qts[c], preferred_element_type=F32) for c in range(2)]

        pending = [scores(steps[d]) for d in range(SCORE_LOOKAHEAD)]
        state = None
        for n, (t, idx) in enumerate(steps):
            i, rows, _, js = tile_ctx(t)
            r = order[idx]
            if idx == 0:
                state = (m0, acc0, m0, acc0)
            s_cur = pending.pop(0)
            if n + SCORE_LOOKAHEAD < len(steps):
                pending.append(scores(steps[n + SCORE_LOOKAHEAD]))
            if r in (0, 1, n_chunks - 1):
                bias = bias_ref[0, jnp.clip(js[idx] - i, -2, 2) + 2]
            else:
                bias = jnp.where(i + r < n_chunks, c_right, c_left)
            state = update(js[idx], s_cur, bias, state)
            if idx == n_chunks - 1:
                a1 = state[1]
                a2 = state[3]
                o = a1[0:ATT_VH] / a1[ATT_VH:ATT_VH + 1] - lam * (a2[0:ATT_VH] / a2[ATT_VH:ATT_VH + 1])
                ms = jnp.mean(o * o, axis=0, keepdims=True)
                y = o * lax.rsqrt(ms + EPS) * g_ref[...] * (1.0 - LAMBDA_INIT)
                o_ref[0, rows, :] = y.T.astype(BF16)
        return carry

    lax.fori_loop(0, n_chunks // TILES_PER_BODY, query_tiles, 0)


def _diff_attn(proj3d, lamv, bias_tiles, subln_col):
    batch, seq, _ = proj3d.shape
    tile = ATT_TILE
    assert seq % tile == 0 and 2 * ATT_DH == LANES and ATT_VH == LANES
    n_chunks = seq // tile
    acc_rows = ATT_VH + BF16_SUBLANES
    k_blk = ATT_HEADS
    v_blk = 2 * ATT_HEADS
    est = (8 * seq * LANES * 2 + 2 * 5 * tile * tile * 4 + n_chunks * acc_rows * tile * 2
           + 2 * acc_rows * tile * 4 + 2 * n_chunks * tile * tile * 4)
    cfar = jnp.stack([bias_tiles[:, 0, 0, 0], bias_tiles[:, 4, 0, 0]], axis=1)
    return pl.pallas_call(
        functools.partial(_diff_attn_kernel, n_chunks=n_chunks),
        grid=(batch, ATT_HEADS),
        in_specs=[
            pl.BlockSpec(memory_space=pltpu.SMEM),
            pl.BlockSpec((4, ATT_DH), lambda b, h: (0, 0)),
            pl.BlockSpec((1, seq, LANES), lambda b, h: (b, 0, h)),
            pl.BlockSpec((1, seq, LANES), lambda b, h: (b, 0, k_blk + h)),
            pl.BlockSpec((1, seq, LANES), lambda b, h: (b, 0, v_blk + h)),
            pl.BlockSpec((1, 5, tile, tile), lambda b, h: (h, 0, 0, 0)),
            pl.BlockSpec((ATT_VH, 1), lambda b, h: (0, 0)),
        ],
        out_specs=pl.BlockSpec((1, seq, LANES), lambda b, h: (b, 0, h)),
        out_shape=jax.ShapeDtypeStruct((batch, seq, ATT_HEADS * ATT_VH), BF16),
        scratch_shapes=[pltpu.VMEM((n_chunks, acc_rows, tile), BF16)],
        compiler_params=pltpu.CompilerParams(
            dimension_semantics=("arbitrary", "arbitrary"),
            vmem_limit_bytes=_vmem_limit(est)),
        name="diff_attn",
    )(cfar, lamv, proj3d, proj3d, proj3d, bias_tiles, subln_col)


def _out_proj_kernel(x_ref, att_ref, b_ref, c_ref, h_ref, cp_ref, hp_ref, cn_ref, hn_ref, cw_ref, w_ref,
                     o_ref, gpad_scr, *, tiles_per_seq):
    tm = x_ref.shape[0]
    att_w = att_ref.shape[1]
    i = pl.program_id(0)
    pos = i % tiles_per_seq
    g = c_ref[...].astype(F32) * h_ref[...].astype(F32)
    g_prev = (cp_ref[...].astype(F32) * hp_ref[...].astype(F32))[BF16_SUBLANES - 1:BF16_SUBLANES]
    g_next = (cn_ref[...].astype(F32) * hn_ref[...].astype(F32))[0:1]
    g_prev = jnp.where(pos == 0, 0.0, g_prev)
    g_next = jnp.where(pos == tiles_per_seq - 1, 0.0, g_next)
    gpad_scr[7:8, :] = g_prev
    gpad_scr[8:8 + tm, :] = g
    gpad_scr[8 + tm:9 + tm, :] = g_next
    cw = cw_ref[...]
    conv = b_ref[...].astype(F32) * (cw[0:1] * gpad_scr[7:7 + tm, :] + cw[1:2] * g
                                     + cw[2:3] * gpad_scr[9:9 + tm, :])
    conv = conv.astype(BF16)
    att = att_ref[...]
    for n0 in range(0, o_ref.shape[1], PROJ_COL_CHUNK):
        cols = slice(n0, n0 + PROJ_COL_CHUNK)
        y = jnp.dot(att, w_ref[0:att_w, cols], preferred_element_type=F32)
        y = y + jnp.dot(conv, w_ref[att_w:, cols], preferred_element_type=F32)
        o_ref[:, cols] = x_ref[:, cols] + y


def _out_proj(x2d, att2d, proj2d, conv_w, w_bf16, *, seq):
    tokens, d = x2d.shape
    att_w = att2d.shape[1]
    conv_width = conv_w.shape[1]
    tm = ROW_TILE
    assert seq % tm == 0 and att_w == conv_width and d % PROJ_COL_CHUNK == 0
    tiles_per_seq = seq // tm
    blk0 = (3 * att_w) // conv_width
    halo = BF16_SUBLANES
    rows_per_tile = tm // halo
    last_halo = tokens // halo - 1
    est = (4 * tm * d * 4 + 8 * tm * conv_width * 2 + 2 * (att_w + conv_width) * d * 2
           + (tm + 16) * conv_width * 4 + 6 * tm * conv_width * 4)
    return pl.pallas_call(
        functools.partial(_out_proj_kernel, tiles_per_seq=tiles_per_seq),
        grid=(tokens // tm,),
        in_specs=[
            pl.BlockSpec((tm, d), lambda i: (i, 0)),
            pl.BlockSpec((tm, att_w), lambda i: (i, 0)),
            pl.BlockSpec((tm, conv_width), lambda i: (i, blk0)),
            pl.BlockSpec((tm, conv_width), lambda i: (i, blk0 + 1)),
            pl.BlockSpec((tm, conv_width), lambda i: (i, blk0 + 2)),
            pl.BlockSpec((halo, conv_width), lambda i: (jnp.maximum(i * rows_per_tile - 1, 0), blk0 + 1)),
            pl.BlockSpec((halo, conv_width), lambda i: (jnp.maximum(i * rows_per_tile - 1, 0), blk0 + 2)),
            pl.BlockSpec((halo, conv_width),
                         lambda i: (jnp.minimum((i + 1) * rows_per_tile, last_halo), blk0 + 1)),
            pl.BlockSpec((halo, conv_width),
                         lambda i: (jnp.minimum((i + 1) * rows_per_tile, last_halo), blk0 + 2)),
            pl.BlockSpec(conv_w.shape, lambda i: (0, 0)),
            pl.BlockSpec(w_bf16.shape, lambda i: (0, 0)),
        ],
        out_specs=pl.BlockSpec((tm, d), lambda i: (i, 0)),
        out_shape=jax.ShapeDtypeStruct((tokens, d), F32),
        scratch_shapes=[pltpu.VMEM((tm + 16, conv_width), F32)],
        compiler_params=pltpu.CompilerParams(
            dimension_semantics=("arbitrary",), vmem_limit_bytes=_vmem_limit(est)),
        name="out_proj",
    )(x2d, att2d, proj2d, proj2d, proj2d, proj2d, proj2d, proj2d, proj2d, conv_w, w_bf16)


def _ffn_kernel(x_ref, g_ref, wg_ref, wu_ref, wd_ref, fg_ref, o_ref, z_scr, hid_scr):
    d = x_ref.shape[1]
    d_ff = wg_ref.shape[1]
    x = x_ref[...]
    ms = jnp.mean(x * x, axis=-1, keepdims=True)
    z_scr[...] = (x * lax.rsqrt(ms + EPS) * g_ref[...]).astype(BF16)
    for f0 in range(0, d_ff, FFN_COL_CHUNK):
        cols = slice(f0, f0 + FFN_COL_CHUNK)
        z = z_scr[...]
        gate = jnp.dot(z, wg_ref[:, cols], preferred_element_type=F32)
        up = jnp.dot(z, wu_ref[:, cols], preferred_element_type=F32)
        hid_scr[:, cols] = (gate * jax.nn.sigmoid(gate) * up).astype(BF16)
    ss = jnp.zeros((x_ref.shape[0], 1), F32)
    for n0 in range(0, d, PROJ_COL_CHUNK):
        cols = slice(n0, n0 + PROJ_COL_CHUNK)
        y = x_ref[:, cols] + jnp.dot(hid_scr[...], wd_ref[:, cols], preferred_element_type=F32)
        o_ref[:, cols] = y
        ss = ss + jnp.sum(y * y, axis=-1, keepdims=True)
    o_ref[...] = o_ref[...] * lax.rsqrt(ss * (1.0 / d) + EPS) * fg_ref[...]


def _ffn(x2d, g, wg, wu, wd, fg):
    tokens, d = x2d.shape
    d_ff = wg.shape[1]
    tm = ROW_TILE
    assert tokens % tm == 0 and d_ff % FFN_COL_CHUNK == 0 and d % PROJ_COL_CHUNK == 0
    est = 4 * tm * d * 4 + 3 * d * d_ff * 2 + tm * (d + d_ff) * 2 + 6 * tm * PROJ_COL_CHUNK * 4
    const = dict(pipeline_mode=pl.Buffered(1))
    return pl.pallas_call(
        _ffn_kernel,
        grid=(tokens // tm,),
        in_specs=[
            pl.BlockSpec((tm, d), lambda i: (i, 0)),
            pl.BlockSpec((1, d), lambda i: (0, 0)),
            pl.BlockSpec((d, d_ff), lambda i: (0, 0), **const),
            pl.BlockSpec((d, d_ff), lambda i: (0, 0), **const),
            pl.BlockSpec((d_ff, d), lambda i: (0, 0), **const),
            pl.BlockSpec((1, d), lambda i: (0, 0)),
        ],
        out_specs=pl.BlockSpec((tm, d), lambda i: (i, 0)),
        out_shape=jax.ShapeDtypeStruct((tokens, d), F32),
        scratch_shapes=[pltpu.VMEM((tm, d), BF16), pltpu.VMEM((tm, d_ff), BF16)],
        compiler_params=pltpu.CompilerParams(
            dimension_semantics=("arbitrary",), vmem_limit_bytes=_vmem_limit(est)),
        name="ffn",
    )(x2d, g, wg, wu, wd, fg)


def kernel(x, norm_mix_g, w_in, lambda_q1, lambda_k1, lambda_q2, lambda_k2, subln_g, rel_bias, conv_w,
           w_out, norm_ffn_g, w_gate, w_up, w_down, final_g):
    batch, seq, d = x.shape
    assert norm_mix_g.shape[0] == 1, "single-layer trunk"
    att_width = ATT_HEADS * ATT_VH
    tokens = batch * seq
    x2d = x.reshape(tokens, d)

    proj = _in_proj(x2d, norm_mix_g[0][None, :], w_in[0].astype(BF16),
                    q_cols=att_width, q_scale=ATT_DH ** -0.5 * LOG2E)

    lamv = jnp.stack([lambda_q1[0], lambda_k1[0], lambda_q2[0], lambda_k2[0]], axis=0).astype(F32)
    att = _diff_attn(proj.reshape(batch, seq, proj.shape[1]), lamv, _bias_tiles(rel_bias, ATT_TILE),
                     subln_g[0].astype(F32)[:, None])

    x1 = _out_proj(x2d, att.reshape(tokens, att_width), proj, conv_w[0].astype(F32),
                   w_out[0].astype(BF16), seq=seq)

    out = _ffn(x1, norm_ffn_g[0][None, :], w_gate[0].astype(BF16), w_up[0].astype(BF16),
               w_down[0].astype(BF16), final_g[None, :])
    return out.reshape(batch, seq, d)
```

```python
import functools
import math

import jax
import jax.numpy as jnp
from jax import lax
from jax.experimental import pallas as pl
from jax.experimental.pallas import tpu as pltpu

F32 = jnp.float32
BF16 = jnp.bfloat16

ATT_HEADS = 4
ATT_DH = 64
ATT_VH = 2 * ATT_DH
N_BUCKETS = 32
MAX_DISTANCE = 128
EPS = 1e-6
LAMBDA_INIT = 0.8 - 0.6 * math.exp(-0.3 * 0)
LOG2E = 1.4426950408889634

LANES = 128
BF16_SUBLANES = 16
VMEM_LIMIT_CAP = 56 * 1024 * 1024

ROW_TILE = 512
ATT_TILE = 256
PROJ_COL_CHUNK = 512
FFN_COL_CHUNK = 256
M_INIT = -1e30
SCORE_LOOKAHEAD = 3
TILES_PER_BODY = 8


def _vmem_limit(nbytes):
    return int(min(max(2 * nbytes, 16 * 1024 * 1024), VMEM_LIMIT_CAP))


def _t5_bucket(rel):
    n_half = N_BUCKETS // 2
    max_exact = n_half // 2
    ret = jnp.where(rel > 0, n_half, 0).astype(jnp.int32)
    n = jnp.abs(rel).astype(jnp.int32)
    nf = jnp.maximum(n, 1).astype(jnp.float32)
    large = max_exact + (jnp.log(nf / max_exact) / math.log(MAX_DISTANCE / max_exact)
                         * (n_half - max_exact)).astype(jnp.int32)
    large = jnp.minimum(large, n_half - 1)
    return ret + jnp.where(n < max_exact, n, large)


def _bias_tiles(rel_bias, tile):
    assert tile >= MAX_DISTANCE
    heads = rel_bias.shape[1]
    rel = jnp.arange(-2 * tile, 2 * tile, dtype=jnp.int32)
    bucket = _t5_bucket(rel)[None]
    rb = rel_bias.astype(F32).T * LOG2E
    table = jnp.zeros((heads, 4 * tile), F32)
    for b in range(N_BUCKETS):
        table = jnp.where(bucket == b, rb[:, b][:, None], table)

    def toeplitz(dj):
        start = dj * tile - (tile - 1) + 2 * tile
        w = table[:, start:start + 2 * tile - 1]
        w = jnp.concatenate([w, jnp.zeros((heads, 1), F32)], axis=1)
        wrapped = jnp.tile(w, (1, tile))[:, :tile * (2 * tile - 1)].reshape(heads, tile, 2 * tile - 1)
        return jnp.swapaxes(wrapped[:, :, tile - 1:], 1, 2)

    def constant(value):
        return jnp.broadcast_to(value[:, None, None], (heads, tile, tile))

    return jnp.stack([constant(table[:, 0]), toeplitz(-1), toeplitz(0), toeplitz(1),
                      constant(table[:, -1])], axis=1)


def _in_proj_kernel(x_ref, g_ref, w_ref, o_ref, *, q_cols, q_scale):
    x = x_ref[...]
    ms = jnp.mean(x * x, axis=-1, keepdims=True)
    u = (x * lax.rsqrt(ms + EPS) * g_ref[...]).astype(BF16)
    n_out = o_ref.shape[1]
    for n0 in range(0, n_out, PROJ_COL_CHUNK):
        r = jnp.dot(u, w_ref[:, n0:n0 + PROJ_COL_CHUNK], preferred_element_type=F32)
        if n0 < q_cols:
            r = r * q_scale
        o_ref[:, n0:n0 + PROJ_COL_CHUNK] = r.astype(BF16)


def _in_proj(x2d, g, w_bf16, *, q_cols, q_scale):
    tokens, d = x2d.shape
    n_out = w_bf16.shape[1]
    assert tokens % ROW_TILE == 0 and q_cols % PROJ_COL_CHUNK == 0 and n_out % PROJ_COL_CHUNK == 0
    est = 2 * ROW_TILE * d * 4 + 2 * d * n_out * 2 + 2 * ROW_TILE * n_out * 2 + ROW_TILE * (d * 2 + PROJ_COL_CHUNK * 8)
    return pl.pallas_call(
        functools.partial(_in_proj_kernel, q_cols=q_cols, q_scale=q_scale),
        grid=(tokens // ROW_TILE,),
        in_specs=[
            pl.BlockSpec((ROW_TILE, d), lambda i: (i, 0)),
            pl.BlockSpec((1, d), lambda i: (0, 0)),
            pl.BlockSpec((d, n_out), lambda i: (0, 0)),
        ],
        out_specs=pl.BlockSpec((ROW_TILE, n_out), lambda i: (i, 0)),
        out_shape=jax.ShapeDtypeStruct((tokens, n_out), BF16),
        compiler_params=pltpu.CompilerParams(
            dimension_semantics=("arbitrary",), vmem_limit_bytes=_vmem_limit(est)),
        name="in_proj",
    )(x2d, g, w_bf16)


def _diff_attn_kernel(cfar_ref, lamv_ref, q_ref, k_ref, v_ref, bias_ref, g_ref, o_ref, vt_scr, *, n_chunks):
    tile = ATT_TILE
    h = pl.program_id(1)

    for j in range(n_chunks):
        vc = v_ref[0, j * tile:(j + 1) * tile, :].astype(F32)
        vt_scr[j, 0:ATT_VH, :] = vc.T.astype(BF16)
        vt_scr[j, ATT_VH:ATT_VH + BF16_SUBLANES, :] = jnp.ones((BF16_SUBLANES, tile), BF16)

    lv = lamv_ref[...]
    lam = (jnp.exp(jnp.sum(lv[0:1] * lv[1:2], axis=-1, keepdims=True))
           - jnp.exp(jnp.sum(lv[2:3] * lv[3:4], axis=-1, keepdims=True)) + LAMBDA_INIT)
    c_left = cfar_ref[h, 0]
    c_right = cfar_ref[h, 1]

    def update(j, s2, bias, state):
        vt = vt_scr[j]
        new_state = []
        for c in range(2):
            m_old, acc = state[2 * c], state[2 * c + 1]
            s = s2[c]
            if bias.ndim == 2:
                s = (bias + s).astype(BF16)
                offset = 0.0
            else:
                s = s.astype(BF16)
                offset = bias
            m_chunk = jnp.max(s, axis=0, keepdims=True).astype(F32) + offset
            shift = (jnp.maximum(m_old, m_chunk) - offset).astype(BF16)
            m_new = shift.astype(F32) + offset
            p = jnp.exp2(s - shift)
            pv = jnp.dot(vt, p, preferred_element_type=F32)
            new_state += [m_new, jnp.exp2(m_old - m_new) * acc + pv]
        return tuple(new_state)

    order = [0, 1, n_chunks - 1] + list(range(2, n_chunks - 1))
    m0 = jnp.full((1, tile), M_INIT, F32)
    acc0 = jnp.zeros((ATT_VH + BF16_SUBLANES, tile), F32)

    def query_tiles(it, carry):
        steps = [(t, idx) for t in range(TILES_PER_BODY) for idx in range(n_chunks)]
        ctx = {}

        def tile_ctx(t):
            if t not in ctx:
                i = it * TILES_PER_BODY + t
                rows = pl.ds(pl.multiple_of(i * tile, tile), tile)
                qt = q_ref[0, rows, :].astype(F32).T
                row = lax.broadcasted_iota(jnp.int32, qt.shape, 0)
                qts = (jnp.where(row < ATT_DH, qt, 0.0).astype(BF16),
                       jnp.where(row >= ATT_DH, qt, 0.0).astype(BF16))
                ctx[t] = (i, rows, qts, [lax.rem(i + r, n_chunks) for r in order])
            return ctx[t]

        def scores(step):
            _, _, qts, js = tile_ctx(step[0])
            kc = k_ref[0, pl.ds(pl.multiple_of(js[step[1]] * tile, tile), tile), :]
            return [jnp.dot(kc, qts[c], preferred_element_type=F32) for c in range(2)]

        pending = [scores(steps[d]) for d in range(SCORE_LOOKAHEAD)]
        state = None
        for n, (t, idx) in enumerate(steps):
            i, rows, _, js = tile_ctx(t)
            r = order[idx]
            if idx == 0:
                state = (m0, acc0, m0, acc0)
            s_cur = pending.pop(0)
            if n + SCORE_LOOKAHEAD < len(steps):
                pending.append(scores(steps[n + SCORE_LOOKAHEAD]))
            if r in (0, 1, n_chunks - 1):
                bias = bias_ref[0, jnp.clip(js[idx] - i, -2, 2) + 2]
            else:
                bias = jnp.where(i + r < n_chunks, c_right, c_left)
            state = update(js[idx], s_cur, bias, state)
            if idx == n_chunks - 1:
                a1 = state[1]
                a2 = state[3]
                o = a1[0:ATT_VH] / a1[ATT_VH:ATT_VH + 1] - lam * (a2[0:ATT_VH] / a2[ATT_VH:ATT_VH + 1])
                ms = jnp.mean(o * o, axis=0, keepdims=True)
                y = o * lax.rsqrt(ms + EPS) * g_ref[...] * (1.0 - LAMBDA_INIT)
                o_ref[0, rows, :] = y.T.astype(BF16)
        return carry

    lax.fori_loop(0, n_chunks // TILES_PER_BODY, query_tiles, 0)


def _diff_attn(proj3d, lamv, bias_tiles, subln_col):
    batch, seq, _ = proj3d.shape
    tile = ATT_TILE
    assert seq % tile == 0 and 2 * ATT_DH == LANES and ATT_VH == LANES
    n_chunks = seq // tile
    assert n_chunks % TILES_PER_BODY == 0 and n_chunks >= 4
    acc_rows = ATT_VH + BF16_SUBLANES
    k_blk = ATT_HEADS
    v_blk = 2 * ATT_HEADS
    est = (8 * seq * LANES * 2 + 2 * 5 * tile * tile * 4 + n_chunks * acc_rows * tile * 2
           + 2 * acc_rows * tile * 4 + 2 * n_chunks * tile * tile * 4)
    cfar = jnp.stack([bias_tiles[:, 0, 0, 0], bias_tiles[:, 4, 0, 0]], axis=1)
    return pl.pallas_call(
        functools.partial(_diff_attn_kernel, n_chunks=n_chunks),
        grid=(batch, ATT_HEADS),
        in_specs=[
            pl.BlockSpec(memory_space=pltpu.SMEM),
            pl.BlockSpec((4, ATT_DH), lambda b, h: (0, 0)),
            pl.BlockSpec((1, seq, LANES), lambda b, h: (b, 0, h)),
            pl.BlockSpec((1, seq, LANES), lambda b, h: (b, 0, k_blk + h)),
            pl.BlockSpec((1, seq, LANES), lambda b, h: (b, 0, v_blk + h)),
            pl.BlockSpec((1, 5, tile, tile), lambda b, h: (h, 0, 0, 0)),
            pl.BlockSpec((ATT_VH, 1), lambda b, h: (0, 0)),
        ],
        out_specs=pl.BlockSpec((1, seq, LANES), lambda b, h: (b, 0, h)),
        out_shape=jax.ShapeDtypeStruct((batch, seq, ATT_HEADS * ATT_VH), BF16),
        scratch_shapes=[pltpu.VMEM((n_chunks, acc_rows, tile), BF16)],
        compiler_params=pltpu.CompilerParams(
            dimension_semantics=("arbitrary", "arbitrary"),
            vmem_limit_bytes=_vmem_limit(est)),
        name="diff_attn",
    )(cfar, lamv, proj3d, proj3d, proj3d, bias_tiles, subln_col)


def _mix_ffn_kernel(x_ref, att_ref, b_ref, c_ref, h_ref, cp_ref, hp_ref, cn_ref, hn_ref, cw_ref, wo_ref,
                    g_ref, wg_ref, wu_ref, wd_ref, fg_ref, o_ref, gpad_scr, x1_scr, z_scr, hid_scr,
                    *, tiles_per_seq):
    tm, d = x_ref.shape
    att_w = att_ref.shape[1]
    d_ff = wg_ref.shape[1]
    pos = pl.program_id(0) % tiles_per_seq

    g = c_ref[...].astype(F32) * h_ref[...].astype(F32)
    g_prev = (cp_ref[...].astype(F32) * hp_ref[...].astype(F32))[BF16_SUBLANES - 1:BF16_SUBLANES]
    g_next = (cn_ref[...].astype(F32) * hn_ref[...].astype(F32))[0:1]
    g_prev = jnp.where(pos == 0, 0.0, g_prev)
    g_next = jnp.where(pos == tiles_per_seq - 1, 0.0, g_next)
    gpad_scr[7:8, :] = g_prev
    gpad_scr[8:8 + tm, :] = g
    gpad_scr[8 + tm:9 + tm, :] = g_next
    cw = cw_ref[...]
    conv = b_ref[...].astype(F32) * (cw[0:1] * gpad_scr[7:7 + tm, :] + cw[1:2] * g
                                     + cw[2:3] * gpad_scr[9:9 + tm, :])
    conv = conv.astype(BF16)
    att = att_ref[...]

    ss = jnp.zeros((tm, 1), F32)
    for n0 in range(0, d, PROJ_COL_CHUNK):
        cols = slice(n0, n0 + PROJ_COL_CHUNK)
        y = jnp.dot(att, wo_ref[0:att_w, cols], preferred_element_type=F32)
        y = y + jnp.dot(conv, wo_ref[att_w:, cols], preferred_element_type=F32)
        x1 = x_ref[:, cols] + y
        x1_scr[:, cols] = x1
        ss = ss + jnp.sum(x1 * x1, axis=-1, keepdims=True)
    z_scr[...] = (x1_scr[...] * lax.rsqrt(ss * (1.0 / d) + EPS) * g_ref[...]).astype(BF16)

    for f0 in range(0, d_ff, FFN_COL_CHUNK):
        cols = slice(f0, f0 + FFN_COL_CHUNK)
        z = z_scr[...]
        gate = jnp.dot(z, wg_ref[:, cols], preferred_element_type=F32)
        up = jnp.dot(z, wu_ref[:, cols], preferred_element_type=F32)
        hid_scr[:, cols] = (gate * jax.nn.sigmoid(gate) * up).astype(BF16)

    ss = jnp.zeros((tm, 1), F32)
    for n0 in range(0, d, PROJ_COL_CHUNK):
        cols = slice(n0, n0 + PROJ_COL_CHUNK)
        y = x1_scr[:, cols] + jnp.dot(hid_scr[...], wd_ref[:, cols], preferred_element_type=F32)
        o_ref[:, cols] = y
        ss = ss + jnp.sum(y * y, axis=-1, keepdims=True)
    o_ref[...] = o_ref[...] * lax.rsqrt(ss * (1.0 / d) + EPS) * fg_ref[...]


def _mix_ffn(x2d, att2d, proj2d, conv_w, wo, g, wg, wu, wd, fg, *, seq):
    tokens, d = x2d.shape
    att_w = att2d.shape[1]
    conv_width = conv_w.shape[1]
    d_ff = wg.shape[1]
    tm = ROW_TILE
    assert seq % tm == 0 and att_w == conv_width
    assert d % PROJ_COL_CHUNK == 0 and d_ff % FFN_COL_CHUNK == 0
    tiles_per_seq = seq // tm
    blk0 = (3 * att_w) // conv_width
    halo = BF16_SUBLANES
    rows_per_tile = tm // halo
    last_halo = tokens // halo - 1
    est = (4 * tm * d * 4 + 8 * tm * conv_width * 2 + (att_w + conv_width) * d * 2 + 3 * d * d_ff * 2
           + (tm + 16) * conv_width * 4 + tm * d * 4 + tm * (d + d_ff) * 2 + 6 * tm * PROJ_COL_CHUNK * 4)
    const = dict(pipeline_mode=pl.Buffered(1))
    prev_blk = lambda i: jnp.maximum(i * rows_per_tile - 1, 0)
    next_blk = lambda i: jnp.minimum((i + 1) * rows_per_tile, last_halo)
    return pl.pallas_call(
        functools.partial(_mix_ffn_kernel, tiles_per_seq=tiles_per_seq),
        grid=(tokens // tm,),
        in_specs=[
            pl.BlockSpec((tm, d), lambda i: (i, 0)),
            pl.BlockSpec((tm, att_w), lambda i: (i, 0)),
            pl.BlockSpec((tm, conv_width), lambda i: (i, blk0)),
            pl.BlockSpec((tm, conv_width), lambda i: (i, blk0 + 1)),
            pl.BlockSpec((tm, conv_width), lambda i: (i, blk0 + 2)),
            pl.BlockSpec((halo, conv_width), lambda i: (prev_blk(i), blk0 + 1)),
            pl.BlockSpec((halo, conv_width), lambda i: (prev_blk(i), blk0 + 2)),
            pl.BlockSpec((halo, conv_width), lambda i: (next_blk(i), blk0 + 1)),
            pl.BlockSpec((halo, conv_width), lambda i: (next_blk(i), blk0 + 2)),
            pl.BlockSpec(conv_w.shape, lambda i: (0, 0)),
            pl.BlockSpec(wo.shape, lambda i: (0, 0), **const),
            pl.BlockSpec((1, d), lambda i: (0, 0)),
            pl.BlockSpec((d, d_ff), lambda i: (0, 0), **const),
            pl.BlockSpec((d, d_ff), lambda i: (0, 0), **const),
            pl.BlockSpec((d_ff, d), lambda i: (0, 0), **const),
            pl.BlockSpec((1, d), lambda i: (0, 0)),
        ],
        out_specs=pl.BlockSpec((tm, d), lambda i: (i, 0)),
        out_shape=jax.ShapeDtypeStruct((tokens, d), F32),
        scratch_shapes=[
            pltpu.VMEM((tm + 16, conv_width), F32),
            pltpu.VMEM((tm, d), F32),
            pltpu.VMEM((tm, d), BF16),
            pltpu.VMEM((tm, d_ff), BF16),
        ],
        compiler_params=pltpu.CompilerParams(
            dimension_semantics=("arbitrary",), vmem_limit_bytes=_vmem_limit(est)),
        name="mix_ffn",
    )(x2d, att2d, proj2d, proj2d, proj2d, proj2d, proj2d, proj2d, proj2d, conv_w, wo, g, wg, wu, wd, fg)


def kernel(x, norm_mix_g, w_in, lambda_q1, lambda_k1, lambda_q2, lambda_k2, subln_g, rel_bias, conv_w,
           w_out, norm_ffn_g, w_gate, w_up, w_down, final_g):
    batch, seq, d = x.shape
    assert norm_mix_g.shape[0] == 1, "single-layer trunk"
    att_width = ATT_HEADS * ATT_VH
    tokens = batch * seq
    x2d = x.reshape(tokens, d)

    proj = _in_proj(x2d, norm_mix_g[0][None, :], w_in[0].astype(BF16),
                    q_cols=att_width, q_scale=ATT_DH ** -0.5 * LOG2E)

    lamv = jnp.stack([lambda_q1[0], lambda_k1[0], lambda_q2[0], lambda_k2[0]], axis=0).astype(F32)
    att = _diff_attn(proj.reshape(batch, seq, proj.shape[1]), lamv, _bias_tiles(rel_bias, ATT_TILE),
                     subln_g[0].astype(F32)[:, None])

    out = _mix_ffn(x2d, att.reshape(tokens, att_width), proj, conv_w[0].astype(F32), w_out[0].astype(BF16),
                   norm_ffn_g[0][None, :], w_gate[0].astype(BF16), w_up[0].astype(BF16),
                   w_down[0].astype(BF16), final_g[None, :], seq=seq)
    return out.reshape(batch, seq, d)
```

```python
import functools
import math

import jax
import jax.numpy as jnp
from jax import lax
from jax.experimental import pallas as pl
from jax.experimental.pallas import tpu as pltpu

F32 = jnp.float32
BF16 = jnp.bfloat16

ATT_HEADS = 4
ATT_DH = 64
ATT_VH = 2 * ATT_DH
N_BUCKETS = 32
MAX_DISTANCE = 128
EPS = 1e-6
LAMBDA_INIT = 0.8 - 0.6 * math.exp(-0.3 * 0)
LOG2E = 1.4426950408889634

LANES = 128
BF16_SUBLANES = 16
VMEM_LIMIT_CAP = 56 * 1024 * 1024

ROW_TILE = 512
IN_PROJ_ROW_TILE = 1024
MIX_ROW_SPLIT = 2
ATT_TILE = 256
PROJ_COL_CHUNK = 512
FFN_COL_CHUNK = 256
M_INIT = -1e30
SCORE_LOOKAHEAD = 3
TILES_PER_BODY = 8


def _vmem_limit(nbytes):
    return int(min(max(2 * nbytes, 16 * 1024 * 1024), VMEM_LIMIT_CAP))


def _t5_bucket(rel):
    n_half = N_BUCKETS // 2
    max_exact = n_half // 2
    ret = jnp.where(rel > 0, n_half, 0).astype(jnp.int32)
    n = jnp.abs(rel).astype(jnp.int32)
    nf = jnp.maximum(n, 1).astype(jnp.float32)
    large = max_exact + (jnp.log(nf / max_exact) / math.log(MAX_DISTANCE / max_exact)
                         * (n_half - max_exact)).astype(jnp.int32)
    large = jnp.minimum(large, n_half - 1)
    return ret + jnp.where(n < max_exact, n, large)


def _bias_tiles(rel_bias, tile):
    assert tile >= MAX_DISTANCE
    heads = rel_bias.shape[1]
    rel = jnp.arange(-2 * tile, 2 * tile, dtype=jnp.int32)
    bucket = _t5_bucket(rel)[None]
    rb = rel_bias.astype(F32).T * LOG2E
    table = jnp.zeros((heads, 4 * tile), F32)
    for b in range(N_BUCKETS):
        table = jnp.where(bucket == b, rb[:, b][:, None], table)

    def toeplitz(dj):
        start = dj * tile - (tile - 1) + 2 * tile
        w = table[:, start:start + 2 * tile - 1]
        w = jnp.concatenate([w, jnp.zeros((heads, 1), F32)], axis=1)
        wrapped = jnp.tile(w, (1, tile))[:, :tile * (2 * tile - 1)].reshape(heads, tile, 2 * tile - 1)
        return jnp.swapaxes(wrapped[:, :, tile - 1:], 1, 2)

    def constant(value):
        return jnp.broadcast_to(value[:, None, None], (heads, tile, tile))

    return jnp.stack([constant(table[:, 0]), toeplitz(-1), toeplitz(0), toeplitz(1),
                      constant(table[:, -1])], axis=1)


def _in_proj_kernel(x_ref, g_ref, w_ref, *refs, q_cols, q_scale, n_side):
    side_in = refs[:n_side]
    o_ref = refs[n_side]
    side_out = refs[n_side + 1:2 * n_side + 1]
    wb_scr = refs[2 * n_side + 1]
    n_out = o_ref.shape[1]

    @pl.when(pl.program_id(0) == 0)
    def _():
        for n0 in range(0, n_out, PROJ_COL_CHUNK):
            wb_scr[:, n0:n0 + PROJ_COL_CHUNK] = w_ref[:, n0:n0 + PROJ_COL_CHUNK].astype(BF16)

    for src_ref, dst_ref in zip(side_in, side_out):
        dst_ref[...] = src_ref[...].astype(BF16)

    x = x_ref[...]
    ms = jnp.mean(x * x, axis=-1, keepdims=True)
    u = (x * lax.rsqrt(ms + EPS) * g_ref[...]).astype(BF16)
    for n0 in range(0, n_out, PROJ_COL_CHUNK):
        r = jnp.dot(u, wb_scr[:, n0:n0 + PROJ_COL_CHUNK], preferred_element_type=F32)
        if n0 < q_cols:
            r = r * q_scale
        o_ref[:, n0:n0 + PROJ_COL_CHUNK] = r.astype(BF16)


def _in_proj(x2d, g, w_f32, side_weights, *, q_cols, q_scale):
    tokens, d = x2d.shape
    n_out = w_f32.shape[1]
    tm = IN_PROJ_ROW_TILE
    steps = tokens // tm
    assert tokens % tm == 0 and q_cols % PROJ_COL_CHUNK == 0 and n_out % PROJ_COL_CHUNK == 0
    side_rows = [w.shape[0] // steps for w in side_weights]
    assert all(w.shape[0] % steps == 0 and r % BF16_SUBLANES == 0 for w, r in zip(side_weights, side_rows))
    side_bytes = sum(r * w.shape[1] * 6 for w, r in zip(side_weights, side_rows))
    est = (2 * tm * d * 4 + d * n_out * 6 + 2 * tm * n_out * 2 + tm * (d * 2 + PROJ_COL_CHUNK * 8)
           + 2 * side_bytes)
    outs = pl.pallas_call(
        functools.partial(_in_proj_kernel, q_cols=q_cols, q_scale=q_scale, n_side=len(side_weights)),
        grid=(steps,),
        in_specs=[
            pl.BlockSpec((tm, d), lambda i: (i, 0)),
            pl.BlockSpec((1, d), lambda i: (0, 0)),
            pl.BlockSpec((d, n_out), lambda i: (0, 0), pipeline_mode=pl.Buffered(1)),
        ] + [pl.BlockSpec((r, w.shape[1]), lambda i: (i, 0)) for w, r in zip(side_weights, side_rows)],
        out_specs=[pl.BlockSpec((tm, n_out), lambda i: (i, 0))]
        + [pl.BlockSpec((r, w.shape[1]), lambda i: (i, 0)) for w, r in zip(side_weights, side_rows)],
        out_shape=[jax.ShapeDtypeStruct((tokens, n_out), BF16)]
        + [jax.ShapeDtypeStruct(w.shape, BF16) for w in side_weights],
        scratch_shapes=[pltpu.VMEM((d, n_out), BF16)],
        compiler_params=pltpu.CompilerParams(
            dimension_semantics=("arbitrary",), vmem_limit_bytes=_vmem_limit(est)),
        name="in_proj",
    )(x2d, g, w_f32, *side_weights)
    return outs[0], outs[1:]


def _diff_attn_kernel(cfar_ref, lamv_ref, q_ref, k_ref, v_ref, bias_ref, g_ref, o_ref, vt_scr, *, n_chunks):
    tile = ATT_TILE
    h = pl.program_id(1)

    for j in range(n_chunks):
        vc = v_ref[0, j * tile:(j + 1) * tile, :].astype(F32)
        vt_scr[j, 0:ATT_VH, :] = vc.T.astype(BF16)
        vt_scr[j, ATT_VH:ATT_VH + BF16_SUBLANES, :] = jnp.ones((BF16_SUBLANES, tile), BF16)

    lv = lamv_ref[...]
    lam = (jnp.exp(jnp.sum(lv[0:1] * lv[1:2], axis=-1, keepdims=True))
           - jnp.exp(jnp.sum(lv[2:3] * lv[3:4], axis=-1, keepdims=True)) + LAMBDA_INIT)
    c_left = cfar_ref[h, 0]
    c_right = cfar_ref[h, 1]

    def update(j, s2, offset, state):
        vt = vt_scr[j]
        new_state = []
        for c in range(2):
            m_old, acc = state[2 * c], state[2 * c + 1]
            s = s2[c]
            m_chunk = jnp.max(s, axis=0, keepdims=True).astype(F32) + offset
            shift = (jnp.maximum(m_old, m_chunk) - offset).astype(BF16)
            m_new = shift.astype(F32) + offset
            p = jnp.exp2(s - shift)
            pv = jnp.dot(vt, p, preferred_element_type=F32)
            new_state += [m_new, jnp.exp2(m_old - m_new) * acc + pv]
        return tuple(new_state)

    order = [0, 1, n_chunks - 1] + list(range(2, n_chunks - 1))
    m0 = jnp.full((1, tile), M_INIT, F32)
    acc0 = jnp.zeros((ATT_VH + BF16_SUBLANES, tile), F32)

    def query_tiles(it, carry):
        steps = [(t, idx) for t in range(TILES_PER_BODY) for idx in range(n_chunks)]
        ctx = {}

        def tile_ctx(t):
            if t not in ctx:
                i = it * TILES_PER_BODY + t
                rows = pl.ds(pl.multiple_of(i * tile, tile), tile)
                qt = q_ref[0, rows, :].astype(F32).T
                row = lax.broadcasted_iota(jnp.int32, qt.shape, 0)
                qts = (jnp.where(row < ATT_DH, qt, 0.0).astype(BF16),
                       jnp.where(row >= ATT_DH, qt, 0.0).astype(BF16))
                ctx[t] = (i, rows, qts, [lax.rem(i + r, n_chunks) for r in order])
            return ctx[t]

        def scores(step):
            i, _, qts, js = tile_ctx(step[0])
            j = js[step[1]]
            kc = k_ref[0, pl.ds(pl.multiple_of(j * tile, tile), tile), :]
            s2 = [jnp.dot(kc, qts[c], preferred_element_type=F32) for c in range(2)]
            if order[step[1]] in (0, 1, n_chunks - 1):
                bias = bias_ref[0, jnp.clip(j - i, -2, 2) + 2]
                s2 = [bias + s for s in s2]
            return [s.astype(BF16) for s in s2]

        pending = [scores(steps[d]) for d in range(SCORE_LOOKAHEAD)]
        state = None
        for n, (t, idx) in enumerate(steps):
            i, rows, _, js = tile_ctx(t)
            r = order[idx]
            if idx == 0:
                state = (m0, acc0, m0, acc0)
            s_cur = pending.pop(0)
            if n + SCORE_LOOKAHEAD < len(steps):
                pending.append(scores(steps[n + SCORE_LOOKAHEAD]))
            if r in (0, 1, n_chunks - 1):
                offset = 0.0
            else:
                offset = jnp.where(i + r < n_chunks, c_right, c_left)
            state = update(js[idx], s_cur, offset, state)
            if idx == n_chunks - 1:
                a1 = state[1]
                a2 = state[3]
                o = a1[0:ATT_VH] / a1[ATT_VH:ATT_VH + 1] - lam * (a2[0:ATT_VH] / a2[ATT_VH:ATT_VH + 1])
                ms = jnp.mean(o * o, axis=0, keepdims=True)
                y = o * lax.rsqrt(ms + EPS) * g_ref[...] * (1.0 - LAMBDA_INIT)
                o_ref[0, rows, :] = y.T.astype(BF16)
        return carry

    lax.fori_loop(0, n_chunks // TILES_PER_BODY, query_tiles, 0)


def _diff_attn(proj3d, lamv, bias_tiles, subln_col):
    batch, seq, _ = proj3d.shape
    tile = ATT_TILE
    assert seq % tile == 0 and 2 * ATT_DH == LANES and ATT_VH == LANES
    n_chunks = seq // tile
    assert n_chunks % TILES_PER_BODY == 0 and n_chunks >= 4
    acc_rows = ATT_VH + BF16_SUBLANES
    k_blk = ATT_HEADS
    v_blk = 2 * ATT_HEADS
    est = (8 * seq * LANES * 2 + 2 * 5 * tile * tile * 4 + n_chunks * acc_rows * tile * 2
           + 2 * acc_rows * tile * 4 + 2 * n_chunks * tile * tile * 4)
    cfar = jnp.stack([bias_tiles[:, 0, 0, 0], bias_tiles[:, 4, 0, 0]], axis=1)
    return pl.pallas_call(
        functools.partial(_diff_attn_kernel, n_chunks=n_chunks),
        grid=(batch, ATT_HEADS),
        in_specs=[
            pl.BlockSpec(memory_space=pltpu.SMEM),
            pl.BlockSpec((4, ATT_DH), lambda b, h: (0, 0)),
            pl.BlockSpec((1, seq, LANES), lambda b, h: (b, 0, h)),
            pl.BlockSpec((1, seq, LANES), lambda b, h: (b, 0, k_blk + h)),
            pl.BlockSpec((1, seq, LANES), lambda b, h: (b, 0, v_blk + h)),
            pl.BlockSpec((1, 5, tile, tile), lambda b, h: (h, 0, 0, 0)),
            pl.BlockSpec((ATT_VH, 1), lambda b, h: (0, 0)),
        ],
        out_specs=pl.BlockSpec((1, seq, LANES), lambda b, h: (b, 0, h)),
        out_shape=jax.ShapeDtypeStruct((batch, seq, ATT_HEADS * ATT_VH), BF16),
        scratch_shapes=[pltpu.VMEM((n_chunks, acc_rows, tile), BF16)],
        compiler_params=pltpu.CompilerParams(
            dimension_semantics=("arbitrary", "arbitrary"),
            vmem_limit_bytes=_vmem_limit(est)),
        name="diff_attn",
    )(cfar, lamv, proj3d, proj3d, proj3d, bias_tiles, subln_col)


def _mix_ffn_kernel(x_ref, att_ref, b_ref, c_ref, h_ref, cp_ref, hp_ref, cn_ref, hn_ref, cw_ref, wo_ref,
                    g_ref, wg_ref, wu_ref, wd_ref, fg_ref, o_ref, gpad_scr, x1_scr, z_scr, hid_scr,
                    *, tiles_per_seq):
    tm, d = x_ref.shape
    att_w = att_ref.shape[1]
    d_ff = wg_ref.shape[1]
    pos = pl.program_id(0) % tiles_per_seq

    g = c_ref[...].astype(F32) * h_ref[...].astype(F32)
    g_prev = (cp_ref[...].astype(F32) * hp_ref[...].astype(F32))[BF16_SUBLANES - 1:BF16_SUBLANES]
    g_next = (cn_ref[...].astype(F32) * hn_ref[...].astype(F32))[0:1]
    g_prev = jnp.where(pos == 0, 0.0, g_prev)
    g_next = jnp.where(pos == tiles_per_seq - 1, 0.0, g_next)
    gpad_scr[7:8, :] = g_prev
    gpad_scr[8:8 + tm, :] = g
    gpad_scr[8 + tm:9 + tm, :] = g_next
    cw = cw_ref[...]
    hm = tm // MIX_ROW_SPLIT

    def mix(k):
        rows = slice(k * hm, (k + 1) * hm)
        lo = 8 + k * hm
        conv = b_ref[rows, :].astype(F32) * (cw[0:1] * gpad_scr[lo - 1:lo - 1 + hm, :]
                                             + cw[1:2] * gpad_scr[lo:lo + hm, :]
                                             + cw[2:3] * gpad_scr[lo + 1:lo + 1 + hm, :])
        conv = conv.astype(BF16)
        att = att_ref[rows, :]
        ss = jnp.zeros((hm, 1), F32)
        for n0 in range(0, d, PROJ_COL_CHUNK):
            cols = slice(n0, n0 + PROJ_COL_CHUNK)
            y = jnp.dot(att, wo_ref[0:att_w, cols], preferred_element_type=F32)
            y = y + jnp.dot(conv, wo_ref[att_w:, cols], preferred_element_type=F32)
            x1 = x_ref[rows, cols] + y
            x1_scr[rows, cols] = x1
            ss = ss + jnp.sum(x1 * x1, axis=-1, keepdims=True)
        z_scr[rows, :] = (x1_scr[rows, :] * lax.rsqrt(ss * (1.0 / d) + EPS) * g_ref[...]).astype(BF16)

    def gate_up(k):
        rows = slice(k * hm, (k + 1) * hm)
        for f0 in range(0, d_ff, FFN_COL_CHUNK):
            cols = slice(f0, f0 + FFN_COL_CHUNK)
            z = z_scr[rows, :]
            gate = jnp.dot(z, wg_ref[:, cols], preferred_element_type=F32)
            up = jnp.dot(z, wu_ref[:, cols], preferred_element_type=F32)
            hid_scr[rows, cols] = (gate * jax.nn.sigmoid(gate) * up).astype(BF16)

    def down(k):
        rows = slice(k * hm, (k + 1) * hm)
        ss = jnp.zeros((hm, 1), F32)
        for n0 in range(0, d, PROJ_COL_CHUNK):
            cols = slice(n0, n0 + PROJ_COL_CHUNK)
            y = x1_scr[rows, cols] + jnp.dot(hid_scr[rows, :], wd_ref[:, cols], preferred_element_type=F32)
            o_ref[rows, cols] = y
            ss = ss + jnp.sum(y * y, axis=-1, keepdims=True)
        return ss

    def final_norm(k, ss):
        rows = slice(k * hm, (k + 1) * hm)
        o_ref[rows, :] = o_ref[rows, :] * lax.rsqrt(ss * (1.0 / d) + EPS) * fg_ref[...]

    for k in range(MIX_ROW_SPLIT):
        mix(k)
    row_ss = None
    for k in range(MIX_ROW_SPLIT):
        gate_up(k)
        if k > 0:
            final_norm(k - 1, row_ss)
        row_ss = down(k)
    final_norm(MIX_ROW_SPLIT - 1, row_ss)


def _mix_ffn(x2d, att2d, proj2d, conv_w, wo, g, wg, wu, wd, fg, *, seq):
    tokens, d = x2d.shape
    att_w = att2d.shape[1]
    conv_width = conv_w.shape[1]
    d_ff = wg.shape[1]
    tm = ROW_TILE
    assert seq % tm == 0 and att_w == conv_width
    assert d % PROJ_COL_CHUNK == 0 and d_ff % FFN_COL_CHUNK == 0
    tiles_per_seq = seq // tm
    blk0 = (3 * att_w) // conv_width
    halo = BF16_SUBLANES
    rows_per_tile = tm // halo
    last_halo = tokens // halo - 1
    est = (4 * tm * d * 4 + 8 * tm * conv_width * 2 + (att_w + conv_width) * d * 2 + 3 * d * d_ff * 2
           + (tm + 16) * conv_width * 4 + tm * d * 4 + tm * (d + d_ff) * 2 + 6 * tm * PROJ_COL_CHUNK * 4)
    const = dict(pipeline_mode=pl.Buffered(1))
    prev_blk = lambda i: jnp.maximum(i * rows_per_tile - 1, 0)
    next_blk = lambda i: jnp.minimum((i + 1) * rows_per_tile, last_halo)
    return pl.pallas_call(
        functools.partial(_mix_ffn_kernel, tiles_per_seq=tiles_per_seq),
        grid=(tokens // tm,),
        in_specs=[
            pl.BlockSpec((tm, d), lambda i: (i, 0)),
            pl.BlockSpec((tm, att_w), lambda i: (i, 0)),
            pl.BlockSpec((tm, conv_width), lambda i: (i, blk0)),
            pl.BlockSpec((tm, conv_width), lambda i: (i, blk0 + 1)),
            pl.BlockSpec((tm, conv_width), lambda i: (i, blk0 + 2)),
            pl.BlockSpec((halo, conv_width), lambda i: (prev_blk(i), blk0 + 1)),
            pl.BlockSpec((halo, conv_width), lambda i: (prev_blk(i), blk0 + 2)),
            pl.BlockSpec((halo, conv_width), lambda i: (next_blk(i), blk0 + 1)),
            pl.BlockSpec((halo, conv_width), lambda i: (next_blk(i), blk0 + 2)),
            pl.BlockSpec(conv_w.shape, lambda i: (0, 0)),
            pl.BlockSpec(wo.shape, lambda i: (0, 0), **const),
            pl.BlockSpec((1, d), lambda i: (0, 0)),
            pl.BlockSpec((d, d_ff), lambda i: (0, 0), **const),
            pl.BlockSpec((d, d_ff), lambda i: (0, 0), **const),
            pl.BlockSpec((d_ff, d), lambda i: (0, 0), **const),
            pl.BlockSpec((1, d), lambda i: (0, 0)),
        ],
        out_specs=pl.BlockSpec((tm, d), lambda i: (i, 0)),
        out_shape=jax.ShapeDtypeStruct((tokens, d), F32),
        scratch_shapes=[
            pltpu.VMEM((tm + 16, conv_width), F32),
            pltpu.VMEM((tm, d), F32),
            pltpu.VMEM((tm, d), BF16),
            pltpu.VMEM((tm, d_ff), BF16),
        ],
        compiler_params=pltpu.CompilerParams(
            dimension_semantics=("arbitrary",), vmem_limit_bytes=_vmem_limit(est)),
        name="mix_ffn",
    )(x2d, att2d, proj2d, proj2d, proj2d, proj2d, proj2d, proj2d, proj2d, conv_w, wo, g, wg, wu, wd, fg)


def kernel(x, norm_mix_g, w_in, lambda_q1, lambda_k1, lambda_q2, lambda_k2, subln_g, rel_bias, conv_w,
           w_out, norm_ffn_g, w_gate, w_up, w_down, final_g):
    batch, seq, d = x.shape
    assert norm_mix_g.shape[0] == 1, "single-layer trunk"
    att_width = ATT_HEADS * ATT_VH
    tokens = batch * seq
    x2d = x.reshape(tokens, d)

    d_ff = w_gate.shape[2]
    proj, (wo, wg, wu, wd) = _in_proj(
        x2d, norm_mix_g[0][None, :], w_in[0].astype(F32),
        [w_out[0].astype(F32), w_gate[0].astype(F32), w_up[0].astype(F32),
         w_down[0].astype(F32).reshape(d, d_ff)],
        q_cols=att_width, q_scale=ATT_DH ** -0.5 * LOG2E)

    lamv = jnp.stack([lambda_q1[0], lambda_k1[0], lambda_q2[0], lambda_k2[0]], axis=0).astype(F32)
    att = _diff_attn(proj.reshape(batch, seq, proj.shape[1]), lamv, _bias_tiles(rel_bias, ATT_TILE),
                     subln_g[0].astype(F32)[:, None])

    out = _mix_ffn(x2d, att.reshape(tokens, att_width), proj, conv_w[0].astype(F32), wo,
                   norm_ffn_g[0][None, :], wg, wu, wd.reshape(d_ff, d), final_g[None, :], seq=seq)
    return out.reshape(batch, seq, d)
```

```python
import functools
import math

import jax
import jax.numpy as jnp
from jax import lax
from jax.experimental import pallas as pl
from jax.experimental.pallas import tpu as pltpu

F32 = jnp.float32
BF16 = jnp.bfloat16

ATT_HEADS = 4
ATT_DH = 64
ATT_VH = 2 * ATT_DH
N_BUCKETS = 32
MAX_DISTANCE = 128
EPS = 1e-6
LAMBDA_INIT = 0.8 - 0.6 * math.exp(-0.3 * 0)
LOG2E = 1.4426950408889634

LANES = 128
BF16_SUBLANES = 16
VMEM_LIMIT_CAP = 56 * 1024 * 1024

ROW_TILE = 512
IN_PROJ_ROW_TILE = 1024
MIX_ROW_SPLIT = 2
ATT_TILE = 256
PROJ_COL_CHUNK = 512
FFN_COL_CHUNK = 256
M_INIT = -1e30
SCORE_LOOKAHEAD = 3
TILES_PER_BODY = 8


def _vmem_limit(nbytes):
    return int(min(max(2 * nbytes, 16 * 1024 * 1024), VMEM_LIMIT_CAP))


def _t5_bucket(rel):
    n_half = N_BUCKETS // 2
    max_exact = n_half // 2
    ret = jnp.where(rel > 0, n_half, 0).astype(jnp.int32)
    n = jnp.abs(rel).astype(jnp.int32)
    nf = jnp.maximum(n, 1).astype(jnp.float32)
    large = max_exact + (jnp.log(nf / max_exact) / math.log(MAX_DISTANCE / max_exact)
                         * (n_half - max_exact)).astype(jnp.int32)
    large = jnp.minimum(large, n_half - 1)
    return ret + jnp.where(n < max_exact, n, large)


def _bias_tiles(rel_bias, tile):
    assert tile >= MAX_DISTANCE
    heads = rel_bias.shape[1]
    rel = jnp.arange(-2 * tile, 2 * tile, dtype=jnp.int32)
    table = rel_bias[_t5_bucket(rel)].astype(F32).T * LOG2E

    def toeplitz(dj):
        start = dj * tile - (tile - 1) + 2 * tile
        w = table[:, start:start + 2 * tile - 1]
        w = jnp.concatenate([w, jnp.zeros((heads, 1), F32)], axis=1)
        wrapped = jnp.tile(w, (1, tile))[:, :tile * (2 * tile - 1)].reshape(heads, tile, 2 * tile - 1)
        return jnp.swapaxes(wrapped[:, :, tile - 1:], 1, 2)

    def constant(value):
        return jnp.broadcast_to(value[:, None, None], (heads, tile, tile))

    return jnp.stack([constant(table[:, 0]), toeplitz(-1), toeplitz(0), toeplitz(1),
                      constant(table[:, -1])], axis=1)


def _in_proj_kernel(x_ref, g_ref, w_ref, *refs, q_cols, q_scale, n_side):
    side_in = refs[:n_side]
    o_ref = refs[n_side]
    side_out = refs[n_side + 1:2 * n_side + 1]
    wb_scr = refs[2 * n_side + 1]
    n_out = o_ref.shape[1]

    @pl.when(pl.program_id(0) == 0)
    def _():
        for n0 in range(0, n_out, PROJ_COL_CHUNK):
            wb_scr[:, n0:n0 + PROJ_COL_CHUNK] = w_ref[:, n0:n0 + PROJ_COL_CHUNK].astype(BF16)

    for src_ref, dst_ref in zip(side_in, side_out):
        dst_ref[...] = src_ref[...].astype(BF16)

    x = x_ref[...]
    ms = jnp.mean(x * x, axis=-1, keepdims=True)
    u = (x * lax.rsqrt(ms + EPS) * g_ref[...]).astype(BF16)
    for n0 in range(0, n_out, PROJ_COL_CHUNK):
        r = jnp.dot(u, wb_scr[:, n0:n0 + PROJ_COL_CHUNK], preferred_element_type=F32)
        if n0 < q_cols:
            r = r * q_scale
        o_ref[:, n0:n0 + PROJ_COL_CHUNK] = r.astype(BF16)


def _in_proj(x2d, g, w_f32, side_weights, *, q_cols, q_scale):
    tokens, d = x2d.shape
    n_out = w_f32.shape[1]
    tm = IN_PROJ_ROW_TILE
    steps = tokens // tm
    assert tokens % tm == 0 and q_cols % PROJ_COL_CHUNK == 0 and n_out % PROJ_COL_CHUNK == 0
    def block_rows(n_rows):
        rows = next(r for r in range(BF16_SUBLANES, n_rows + 1, BF16_SUBLANES)
                    if n_rows % r == 0 and n_rows // r <= steps)
        return rows

    side_rows = [block_rows(w.shape[0]) for w in side_weights]
    side_specs = [pl.BlockSpec((r, w.shape[1]), lambda i, last=w.shape[0] // r - 1: (jnp.minimum(i, last), 0))
                  for w, r in zip(side_weights, side_rows)]
    side_bytes = sum(r * w.shape[1] * 6 for w, r in zip(side_weights, side_rows))
    est = (2 * tm * d * 4 + d * n_out * 6 + 2 * tm * n_out * 2 + tm * (d * 2 + PROJ_COL_CHUNK * 8)
           + 2 * side_bytes)
    outs = pl.pallas_call(
        functools.partial(_in_proj_kernel, q_cols=q_cols, q_scale=q_scale, n_side=len(side_weights)),
        grid=(steps,),
        in_specs=[
            pl.BlockSpec((tm, d), lambda i: (i, 0)),
            pl.BlockSpec((1, d), lambda i: (0, 0)),
            pl.BlockSpec((d, n_out), lambda i: (0, 0), pipeline_mode=pl.Buffered(1)),
        ] + side_specs,
        out_specs=[pl.BlockSpec((tm, n_out), lambda i: (i, 0))] + side_specs,
        out_shape=[jax.ShapeDtypeStruct((tokens, n_out), BF16)]
        + [jax.ShapeDtypeStruct(w.shape, BF16) for w in side_weights],
        scratch_shapes=[pltpu.VMEM((d, n_out), BF16)],
        compiler_params=pltpu.CompilerParams(
            dimension_semantics=("arbitrary",), vmem_limit_bytes=_vmem_limit(est)),
        name="in_proj",
    )(x2d, g, w_f32, *side_weights)
    return outs[0], outs[1:]


def _diff_attn_kernel(cfar_ref, lamv_ref, q_ref, k_ref, v_ref, bias_ref, g_ref, o_ref, vt_scr, *, n_chunks):
    tile = ATT_TILE
    h = pl.program_id(1)

    for j in range(n_chunks):
        vc = v_ref[0, j * tile:(j + 1) * tile, :].astype(F32)
        vt_scr[j, 0:ATT_VH, :] = vc.T.astype(BF16)
        vt_scr[j, ATT_VH:ATT_VH + BF16_SUBLANES, :] = jnp.ones((BF16_SUBLANES, tile), BF16)

    lv = lamv_ref[...]
    lam = (jnp.exp(jnp.sum(lv[0:1] * lv[1:2], axis=-1, keepdims=True))
           - jnp.exp(jnp.sum(lv[2:3] * lv[3:4], axis=-1, keepdims=True)) + LAMBDA_INIT)
    c_left = cfar_ref[h, 0]
    c_right = cfar_ref[h, 1]

    def update(j, s2, offset, state):
        vt = vt_scr[j]
        new_state = []
        for c in range(2):
            m_old, acc = state[2 * c], state[2 * c + 1]
            s = s2[c]
            m_chunk = jnp.max(s, axis=0, keepdims=True).astype(F32) + offset
            shift = (jnp.maximum(m_old, m_chunk) - offset).astype(BF16)
            m_new = shift.astype(F32) + offset
            p = jnp.exp2(s - shift)
            pv = jnp.dot(vt, p, preferred_element_type=F32)
            new_state += [m_new, jnp.exp2(m_old - m_new) * acc + pv]
        return tuple(new_state)

    order = [0, 1, n_chunks - 1] + list(range(2, n_chunks - 1))
    m0 = jnp.full((1, tile), M_INIT, F32)
    acc0 = jnp.zeros((ATT_VH + BF16_SUBLANES, tile), F32)

    def query_tiles(it, carry):
        steps = [(t, idx) for t in range(TILES_PER_BODY) for idx in range(n_chunks)]
        ctx = {}

        def tile_ctx(t):
            if t not in ctx:
                i = it * TILES_PER_BODY + t
                rows = pl.ds(pl.multiple_of(i * tile, tile), tile)
                qt = q_ref[0, rows, :].astype(F32).T
                row = lax.broadcasted_iota(jnp.int32, qt.shape, 0)
                qts = (jnp.where(row < ATT_DH, qt, 0.0).astype(BF16),
                       jnp.where(row >= ATT_DH, qt, 0.0).astype(BF16))
                ctx[t] = (i, rows, qts, [lax.rem(i + r, n_chunks) for r in order])
            return ctx[t]

        def scores(step):
            i, _, qts, js = tile_ctx(step[0])
            j = js[step[1]]
            kc = k_ref[0, pl.ds(pl.multiple_of(j * tile, tile), tile), :]
            s2 = [jnp.dot(kc, qts[c], preferred_element_type=F32) for c in range(2)]
            if order[step[1]] in (0, 1, n_chunks - 1):
                bias = bias_ref[0, jnp.clip(j - i, -2, 2) + 2]
                s2 = [bias + s for s in s2]
            return [s.astype(BF16) for s in s2]

        pending = [scores(steps[d]) for d in range(SCORE_LOOKAHEAD)]
        state = None
        for n, (t, idx) in enumerate(steps):
            i, rows, _, js = tile_ctx(t)
            r = order[idx]
            if idx == 0:
                state = (m0, acc0, m0, acc0)
            s_cur = pending.pop(0)
            if n + SCORE_LOOKAHEAD < len(steps):
                pending.append(scores(steps[n + SCORE_LOOKAHEAD]))
            if r in (0, 1, n_chunks - 1):
                offset = 0.0
            else:
                offset = jnp.where(i + r < n_chunks, c_right, c_left)
            state = update(js[idx], s_cur, offset, state)
            if idx == n_chunks - 1:
                a1 = state[1]
                a2 = state[3]
                o = a1[0:ATT_VH] / a1[ATT_VH:ATT_VH + 1] - lam * (a2[0:ATT_VH] / a2[ATT_VH:ATT_VH + 1])
                ms = jnp.mean(o * o, axis=0, keepdims=True)
                y = o * lax.rsqrt(ms + EPS) * g_ref[...] * (1.0 - LAMBDA_INIT)
                o_ref[0, rows, :] = y.T.astype(BF16)
        return carry

    lax.fori_loop(0, n_chunks // TILES_PER_BODY, query_tiles, 0)


def _diff_attn(proj3d, lamv, bias_tiles, subln_col):
    batch, seq, _ = proj3d.shape
    tile = ATT_TILE
    assert seq % tile == 0 and 2 * ATT_DH == LANES and ATT_VH == LANES
    n_chunks = seq // tile
    assert n_chunks % TILES_PER_BODY == 0 and n_chunks >= 4
    acc_rows = ATT_VH + BF16_SUBLANES
    k_blk = ATT_HEADS
    v_blk = 2 * ATT_HEADS
    est = (8 * seq * LANES * 2 + 2 * 5 * tile * tile * 4 + n_chunks * acc_rows * tile * 2
           + 2 * acc_rows * tile * 4 + 2 * n_chunks * tile * tile * 4)
    cfar = jnp.stack([bias_tiles[:, 0, 0, 0], bias_tiles[:, 4, 0, 0]], axis=1)
    return pl.pallas_call(
        functools.partial(_diff_attn_kernel, n_chunks=n_chunks),
        grid=(batch, ATT_HEADS),
        in_specs=[
            pl.BlockSpec(memory_space=pltpu.SMEM),
            pl.BlockSpec((4, ATT_DH), lambda b, h: (0, 0)),
            pl.BlockSpec((1, seq, LANES), lambda b, h: (b, 0, h)),
            pl.BlockSpec((1, seq, LANES), lambda b, h: (b, 0, k_blk + h)),
            pl.BlockSpec((1, seq, LANES), lambda b, h: (b, 0, v_blk + h)),
            pl.BlockSpec((1, 5, tile, tile), lambda b, h: (h, 0, 0, 0)),
            pl.BlockSpec((ATT_VH, 1), lambda b, h: (0, 0)),
        ],
        out_specs=pl.BlockSpec((1, seq, LANES), lambda b, h: (b, 0, h)),
        out_shape=jax.ShapeDtypeStruct((batch, seq, ATT_HEADS * ATT_VH), BF16),
        scratch_shapes=[pltpu.VMEM((n_chunks, acc_rows, tile), BF16)],
        compiler_params=pltpu.CompilerParams(
            dimension_semantics=("arbitrary", "arbitrary"),
            vmem_limit_bytes=_vmem_limit(est)),
        name="diff_attn",
    )(cfar, lamv, proj3d, proj3d, proj3d, bias_tiles, subln_col)


def _mix_ffn_kernel(x_ref, att_ref, b_ref, c_ref, h_ref, cp_ref, hp_ref, cn_ref, hn_ref, cw_ref, wo_ref,
                    g_ref, wg_ref, wu_ref, wd_ref, fg_ref, o_ref, gpad_scr, x1_scr, z_scr, hid_scr,
                    *, tiles_per_seq):
    tm, d = x_ref.shape
    att_w = att_ref.shape[1]
    d_ff = wg_ref.shape[1]
    pos = pl.program_id(0) % tiles_per_seq

    g = c_ref[...].astype(F32) * h_ref[...].astype(F32)
    g_prev = (cp_ref[...].astype(F32) * hp_ref[...].astype(F32))[BF16_SUBLANES - 1:BF16_SUBLANES]
    g_next = (cn_ref[...].astype(F32) * hn_ref[...].astype(F32))[0:1]
    g_prev = jnp.where(pos == 0, 0.0, g_prev)
    g_next = jnp.where(pos == tiles_per_seq - 1, 0.0, g_next)
    gpad_scr[7:8, :] = g_prev
    gpad_scr[8:8 + tm, :] = g
    gpad_scr[8 + tm:9 + tm, :] = g_next
    cw = cw_ref[...]
    hm = tm // MIX_ROW_SPLIT

    def mix(k):
        rows = slice(k * hm, (k + 1) * hm)
        lo = 8 + k * hm
        conv = b_ref[rows, :].astype(F32) * (cw[0:1] * gpad_scr[lo - 1:lo - 1 + hm, :]
                                             + cw[1:2] * gpad_scr[lo:lo + hm, :]
                                             + cw[2:3] * gpad_scr[lo + 1:lo + 1 + hm, :])
        conv = conv.astype(BF16)
        att = att_ref[rows, :]
        ss = jnp.zeros((hm, 1), F32)
        for n0 in range(0, d, PROJ_COL_CHUNK):
            cols = slice(n0, n0 + PROJ_COL_CHUNK)
            y = jnp.dot(att, wo_ref[0:att_w, cols], preferred_element_type=F32)
            y = y + jnp.dot(conv, wo_ref[att_w:, cols], preferred_element_type=F32)
            x1 = x_ref[rows, cols] + y
            x1_scr[rows, cols] = x1
            ss = ss + jnp.sum(x1 * x1, axis=-1, keepdims=True)
        z_scr[rows, :] = (x1_scr[rows, :] * lax.rsqrt(ss * (1.0 / d) + EPS) * g_ref[...]).astype(BF16)

    def gate_up(k):
        rows = slice(k * hm, (k + 1) * hm)
        for f0 in range(0, d_ff, FFN_COL_CHUNK):
            cols = slice(f0, f0 + FFN_COL_CHUNK)
            z = z_scr[rows, :]
            gate = jnp.dot(z, wg_ref[:, cols], preferred_element_type=F32)
            up = jnp.dot(z, wu_ref[:, cols], preferred_element_type=F32)
            hid_scr[rows, cols] = (gate * jax.nn.sigmoid(gate) * up).astype(BF16)

    def down(k):
        rows = slice(k * hm, (k + 1) * hm)
        ss = jnp.zeros((hm, 1), F32)
        for n0 in range(0, d, PROJ_COL_CHUNK):
            cols = slice(n0, n0 + PROJ_COL_CHUNK)
            y = x1_scr[rows, cols] + jnp.dot(hid_scr[rows, :], wd_ref[:, cols], preferred_element_type=F32)
            o_ref[rows, cols] = y
            ss = ss + jnp.sum(y * y, axis=-1, keepdims=True)
        return ss

    def final_norm(k, ss):
        rows = slice(k * hm, (k + 1) * hm)
        o_ref[rows, :] = o_ref[rows, :] * lax.rsqrt(ss * (1.0 / d) + EPS) * fg_ref[...]

    for k in range(MIX_ROW_SPLIT):
        mix(k)
    row_ss = None
    for k in range(MIX_ROW_SPLIT):
        gate_up(k)
        if k > 0:
            final_norm(k - 1, row_ss)
        row_ss = down(k)
    final_norm(MIX_ROW_SPLIT - 1, row_ss)


def _mix_ffn(x2d, att2d, proj2d, conv_w, wo, g, wg, wu, wd, fg, *, seq):
    tokens, d = x2d.shape
    att_w = att2d.shape[1]
    conv_width = conv_w.shape[1]
    d_ff = wg.shape[1]
    tm = ROW_TILE
    assert seq % tm == 0 and att_w == conv_width
    assert d % PROJ_COL_CHUNK == 0 and d_ff % FFN_COL_CHUNK == 0
    tiles_per_seq = seq // tm
    blk0 = (3 * att_w) // conv_width
    halo = BF16_SUBLANES
    rows_per_tile = tm // halo
    last_halo = tokens // halo - 1
    est = (4 * tm * d * 4 + 8 * tm * conv_width * 2 + (att_w + conv_width) * d * 2 + 3 * d * d_ff * 2
           + (tm + 16) * conv_width * 4 + tm * d * 4 + tm * (d + d_ff) * 2 + 6 * tm * PROJ_COL_CHUNK * 4)
    const = dict(pipeline_mode=pl.Buffered(1))
    prev_blk = lambda i: jnp.maximum(i * rows_per_tile - 1, 0)
    next_blk = lambda i: jnp.minimum((i + 1) * rows_per_tile, last_halo)
    return pl.pallas_call(
        functools.partial(_mix_ffn_kernel, tiles_per_seq=tiles_per_seq),
        grid=(tokens // tm,),
        in_specs=[
            pl.BlockSpec((tm, d), lambda i: (i, 0)),
            pl.BlockSpec((tm, att_w), lambda i: (i, 0)),
            pl.BlockSpec((tm, conv_width), lambda i: (i, blk0)),
            pl.BlockSpec((tm, conv_width), lambda i: (i, blk0 + 1)),
            pl.BlockSpec((tm, conv_width), lambda i: (i, blk0 + 2)),
            pl.BlockSpec((halo, conv_width), lambda i: (prev_blk(i), blk0 + 1)),
            pl.BlockSpec((halo, conv_width), lambda i: (prev_blk(i), blk0 + 2)),
            pl.BlockSpec((halo, conv_width), lambda i: (next_blk(i), blk0 + 1)),
            pl.BlockSpec((halo, conv_width), lambda i: (next_blk(i), blk0 + 2)),
            pl.BlockSpec(conv_w.shape, lambda i: (0, 0)),
            pl.BlockSpec(wo.shape, lambda i: (0, 0), **const),
            pl.BlockSpec((1, d), lambda i: (0, 0)),
            pl.BlockSpec((d, d_ff), lambda i: (0, 0), **const),
            pl.BlockSpec((d, d_ff), lambda i: (0, 0), **const),
            pl.BlockSpec((d_ff, d), lambda i: (0, 0), **const),
            pl.BlockSpec((1, d), lambda i: (0, 0)),
        ],
        out_specs=pl.BlockSpec((tm, d), lambda i: (i, 0)),
        out_shape=jax.ShapeDtypeStruct((tokens, d), F32),
        scratch_shapes=[
            pltpu.VMEM((tm + 16, conv_width), F32),
            pltpu.VMEM((tm, d), F32),
            pltpu.VMEM((tm, d), BF16),
            pltpu.VMEM((tm, d_ff), BF16),
        ],
        compiler_params=pltpu.CompilerParams(
            dimension_semantics=("arbitrary",), vmem_limit_bytes=_vmem_limit(est)),
        name="mix_ffn",
    )(x2d, att2d, proj2d, proj2d, proj2d, proj2d, proj2d, proj2d, proj2d, conv_w, wo, g, wg, wu, wd, fg)


def kernel(x, norm_mix_g, w_in, lambda_q1, lambda_k1, lambda_q2, lambda_k2, subln_g, rel_bias, conv_w,
           w_out, norm_ffn_g, w_gate, w_up, w_down, final_g):
    batch, seq, d = x.shape
    assert norm_mix_g.shape[0] == 1, "single-layer trunk"
    att_width = ATT_HEADS * ATT_VH
    tokens = batch * seq
    x2d = x.reshape(tokens, d)

    proj, (wo, wg, wu, wd) = _in_proj(
        x2d, norm_mix_g[0][None, :], w_in[0].astype(F32),
        [w_out[0].astype(F32), w_gate[0].astype(F32), w_up[0].astype(F32), w_down[0].astype(F32)],
        q_cols=att_width, q_scale=ATT_DH ** -0.5 * LOG2E)

    lamv = jnp.stack([lambda_q1[0], lambda_k1[0], lambda_q2[0], lambda_k2[0]], axis=0).astype(F32)
    att = _diff_attn(proj.reshape(batch, seq, proj.shape[1]), lamv, _bias_tiles(rel_bias, ATT_TILE),
                     subln_g[0].astype(F32)[:, None])

    out = _mix_ffn(x2d, att.reshape(tokens, att_width), proj, conv_w[0].astype(F32), wo,
                   norm_ffn_g[0][None, :], wg, wu, wd, final_g[None, :], seq=seq)
    return out.reshape(batch, seq, d)
```

```python
import functools
import math

import jax
import jax.numpy as jnp
from jax import lax
from jax.experimental import pallas as pl
from jax.experimental.pallas import tpu as pltpu

F32 = jnp.float32
BF16 = jnp.bfloat16

ATT_HEADS = 4
ATT_DH = 64
ATT_VH = 2 * ATT_DH
N_BUCKETS = 32
MAX_DISTANCE = 128
EPS = 1e-6
LAMBDA_INIT = 0.8 - 0.6 * math.exp(-0.3 * 0)
LOG2E = 1.4426950408889634

LANES = 128
BF16_SUBLANES = 16
VMEM_LIMIT_CAP = 56 * 1024 * 1024

ROW_TILE = 512
IN_PROJ_ROW_TILE = 1024
MIX_ROW_SPLIT = 2
ATT_TILE = 256
PROJ_COL_CHUNK = 512
FFN_COL_CHUNK = 256
M_INIT = -1e30
SCORE_LOOKAHEAD = 3
TILES_PER_BODY = 16


def _vmem_limit(nbytes):
    return int(min(max(2 * nbytes, 16 * 1024 * 1024), VMEM_LIMIT_CAP))


def _t5_bucket(rel):
    n_half = N_BUCKETS // 2
    max_exact = n_half // 2
    ret = jnp.where(rel > 0, n_half, 0).astype(jnp.int32)
    n = jnp.abs(rel).astype(jnp.int32)
    nf = jnp.maximum(n, 1).astype(jnp.float32)
    large = max_exact + (jnp.log(nf / max_exact) / math.log(MAX_DISTANCE / max_exact)
                         * (n_half - max_exact)).astype(jnp.int32)
    large = jnp.minimum(large, n_half - 1)
    return ret + jnp.where(n < max_exact, n, large)


def _bias_tiles(rel_bias, tile):
    assert tile >= MAX_DISTANCE
    heads = rel_bias.shape[1]
    rel = jnp.arange(-2 * tile, 2 * tile, dtype=jnp.int32)
    table = rel_bias[_t5_bucket(rel)].astype(F32).T * LOG2E

    def toeplitz(dj):
        start = dj * tile - (tile - 1) + 2 * tile
        w = table[:, start:start + 2 * tile - 1]
        w = jnp.concatenate([w, jnp.zeros((heads, 1), F32)], axis=1)
        wrapped = jnp.tile(w, (1, tile))[:, :tile * (2 * tile - 1)].reshape(heads, tile, 2 * tile - 1)
        return jnp.swapaxes(wrapped[:, :, tile - 1:], 1, 2)

    def constant(value):
        return jnp.broadcast_to(value[:, None, None], (heads, tile, tile))

    return jnp.stack([constant(table[:, 0]), toeplitz(-1), toeplitz(0), toeplitz(1),
                      constant(table[:, -1])], axis=1)


def _in_proj_kernel(x_ref, g_ref, w_ref, *refs, q_cols, q_scale, n_side):
    side_in = refs[:n_side]
    o_ref = refs[n_side]
    side_out = refs[n_side + 1:2 * n_side + 1]
    wb_scr = refs[2 * n_side + 1]
    n_out = o_ref.shape[1]

    @pl.when(pl.program_id(0) == 0)
    def _():
        for n0 in range(0, n_out, PROJ_COL_CHUNK):
            wb_scr[:, n0:n0 + PROJ_COL_CHUNK] = w_ref[:, n0:n0 + PROJ_COL_CHUNK].astype(BF16)

    for src_ref, dst_ref in zip(side_in, side_out):
        dst_ref[...] = src_ref[...].astype(BF16)

    x = x_ref[...]
    ms = jnp.mean(x * x, axis=-1, keepdims=True)
    u = (x * lax.rsqrt(ms + EPS) * g_ref[...]).astype(BF16)
    for n0 in range(0, n_out, PROJ_COL_CHUNK):
        r = jnp.dot(u, wb_scr[:, n0:n0 + PROJ_COL_CHUNK], preferred_element_type=F32)
        if n0 < q_cols:
            r = r * q_scale
        o_ref[:, n0:n0 + PROJ_COL_CHUNK] = r.astype(BF16)


def _in_proj(x2d, g, w_f32, side_weights, *, q_cols, q_scale):
    tokens, d = x2d.shape
    n_out = w_f32.shape[1]
    tm = IN_PROJ_ROW_TILE
    steps = tokens // tm
    assert tokens % tm == 0 and q_cols % PROJ_COL_CHUNK == 0 and n_out % PROJ_COL_CHUNK == 0
    def block_rows(n_rows):
        rows = next(r for r in range(BF16_SUBLANES, n_rows + 1, BF16_SUBLANES)
                    if n_rows % r == 0 and n_rows // r <= steps)
        return rows

    side_rows = [block_rows(w.shape[0]) for w in side_weights]
    side_specs = [pl.BlockSpec((r, w.shape[1]), lambda i, last=w.shape[0] // r - 1: (jnp.minimum(i, last), 0))
                  for w, r in zip(side_weights, side_rows)]
    side_bytes = sum(r * w.shape[1] * 6 for w, r in zip(side_weights, side_rows))
    est = (2 * tm * d * 4 + d * n_out * 6 + 2 * tm * n_out * 2 + tm * (d * 2 + PROJ_COL_CHUNK * 8)
           + 2 * side_bytes)
    outs = pl.pallas_call(
        functools.partial(_in_proj_kernel, q_cols=q_cols, q_scale=q_scale, n_side=len(side_weights)),
        grid=(steps,),
        in_specs=[
            pl.BlockSpec((tm, d), lambda i: (i, 0)),
            pl.BlockSpec((1, d), lambda i: (0, 0)),
            pl.BlockSpec((d, n_out), lambda i: (0, 0), pipeline_mode=pl.Buffered(1)),
        ] + side_specs,
        out_specs=[pl.BlockSpec((tm, n_out), lambda i: (i, 0))] + side_specs,
        out_shape=[jax.ShapeDtypeStruct((tokens, n_out), BF16)]
        + [jax.ShapeDtypeStruct(w.shape, BF16) for w in side_weights],
        scratch_shapes=[pltpu.VMEM((d, n_out), BF16)],
        compiler_params=pltpu.CompilerParams(
            dimension_semantics=("arbitrary",), vmem_limit_bytes=_vmem_limit(est)),
        name="in_proj",
    )(x2d, g, w_f32, *side_weights)
    return outs[0], outs[1:]


def _diff_attn_kernel(cfar_ref, lamv_ref, q_ref, k_ref, v_ref, bias_ref, g_ref, o_ref, vt_scr, *, n_chunks):
    tile = ATT_TILE
    h = pl.program_id(1)

    for j in range(n_chunks):
        vc = v_ref[0, j * tile:(j + 1) * tile, :].astype(F32)
        vt_scr[j, 0:ATT_VH, :] = vc.T.astype(BF16)
        vt_scr[j, ATT_VH:ATT_VH + BF16_SUBLANES, :] = jnp.ones((BF16_SUBLANES, tile), BF16)

    lv = lamv_ref[...]
    lam = (jnp.exp(jnp.sum(lv[0:1] * lv[1:2], axis=-1, keepdims=True))
           - jnp.exp(jnp.sum(lv[2:3] * lv[3:4], axis=-1, keepdims=True)) + LAMBDA_INIT)
    c_left = cfar_ref[h, 0]
    c_right = cfar_ref[h, 1]

    def update(j, s2, offset, state):
        vt = vt_scr[j]
        new_state = []
        for c in range(2):
            m_old, acc = state[2 * c], state[2 * c + 1]
            s = s2[c]
            m_chunk = jnp.max(s, axis=0, keepdims=True).astype(F32) + offset
            shift = (jnp.maximum(m_old, m_chunk) - offset).astype(BF16)
            m_new = shift.astype(F32) + offset
            p = jnp.exp2(s - shift)
            pv = jnp.dot(vt, p, preferred_element_type=F32)
            new_state += [m_new, jnp.exp2(m_old - m_new) * acc + pv]
        return tuple(new_state)

    order = [0, 1, n_chunks - 1] + list(range(2, n_chunks - 1))
    m0 = jnp.full((1, tile), M_INIT, F32)
    acc0 = jnp.zeros((ATT_VH + BF16_SUBLANES, tile), F32)

    def query_tiles(it, carry):
        steps = [(t, idx) for t in range(TILES_PER_BODY) for idx in range(n_chunks)]
        ctx = {}

        def tile_ctx(t):
            if t not in ctx:
                i = it * TILES_PER_BODY + t
                rows = pl.ds(pl.multiple_of(i * tile, tile), tile)
                qt = q_ref[0, rows, :].astype(F32).T
                row = lax.broadcasted_iota(jnp.int32, qt.shape, 0)
                qts = (jnp.where(row < ATT_DH, qt, 0.0).astype(BF16),
                       jnp.where(row >= ATT_DH, qt, 0.0).astype(BF16))
                ctx[t] = (i, rows, qts, [lax.rem(i + r, n_chunks) for r in order])
            return ctx[t]

        def scores(step):
            i, _, qts, js = tile_ctx(step[0])
            j = js[step[1]]
            kc = k_ref[0, pl.ds(pl.multiple_of(j * tile, tile), tile), :]
            s2 = [jnp.dot(kc, qts[c], preferred_element_type=F32) for c in range(2)]
            if order[step[1]] in (0, 1, n_chunks - 1):
                bias = bias_ref[0, jnp.clip(j - i, -2, 2) + 2]
                s2 = [bias + s for s in s2]
            return [s.astype(BF16) for s in s2]

        pending = [scores(steps[d]) for d in range(SCORE_LOOKAHEAD)]
        state = None
        for n, (t, idx) in enumerate(steps):
            i, rows, _, js = tile_ctx(t)
            r = order[idx]
            if idx == 0:
                state = (m0, acc0, m0, acc0)
            s_cur = pending.pop(0)
            if n + SCORE_LOOKAHEAD < len(steps):
                pending.append(scores(steps[n + SCORE_LOOKAHEAD]))
            if r in (0, 1, n_chunks - 1):
                offset = 0.0
            else:
                offset = jnp.where(i + r < n_chunks, c_right, c_left)
            state = update(js[idx], s_cur, offset, state)
            if idx == n_chunks - 1:
                a1 = state[1]
                a2 = state[3]
                o = a1[0:ATT_VH] / a1[ATT_VH:ATT_VH + 1] - lam * (a2[0:ATT_VH] / a2[ATT_VH:ATT_VH + 1])
                ms = jnp.mean(o * o, axis=0, keepdims=True)
                y = o * lax.rsqrt(ms + EPS) * g_ref[...] * (1.0 - LAMBDA_INIT)
                o_ref[0, rows, :] = y.T.astype(BF16)
        return carry

    lax.fori_loop(0, n_chunks // TILES_PER_BODY, query_tiles, 0)


def _diff_attn(proj3d, lamv, bias_tiles, subln_col):
    batch, seq, _ = proj3d.shape
    tile = ATT_TILE
    assert seq % tile == 0 and 2 * ATT_DH == LANES and ATT_VH == LANES
    n_chunks = seq // tile
    assert n_chunks % TILES_PER_BODY == 0 and n_chunks >= 4
    acc_rows = ATT_VH + BF16_SUBLANES
    k_blk = ATT_HEADS
    v_blk = 2 * ATT_HEADS
    est = (8 * seq * LANES * 2 + 2 * 5 * tile * tile * 4 + n_chunks * acc_rows * tile * 2
           + 2 * acc_rows * tile * 4 + 2 * n_chunks * tile * tile * 4)
    cfar = jnp.stack([bias_tiles[:, 0, 0, 0], bias_tiles[:, 4, 0, 0]], axis=1)
    return pl.pallas_call(
        functools.partial(_diff_attn_kernel, n_chunks=n_chunks),
        grid=(batch, ATT_HEADS),
        in_specs=[
            pl.BlockSpec(memory_space=pltpu.SMEM),
            pl.BlockSpec((4, ATT_DH), lambda b, h: (0, 0)),
            pl.BlockSpec((1, seq, LANES), lambda b, h: (b, 0, h)),
            pl.BlockSpec((1, seq, LANES), lambda b, h: (b, 0, k_blk + h)),
            pl.BlockSpec((1, seq, LANES), lambda b, h: (b, 0, v_blk + h)),
            pl.BlockSpec((1, 5, tile, tile), lambda b, h: (h, 0, 0, 0)),
            pl.BlockSpec((ATT_VH, 1), lambda b, h: (0, 0)),
        ],
        out_specs=pl.BlockSpec((1, seq, LANES), lambda b, h: (b, 0, h)),
        out_shape=jax.ShapeDtypeStruct((batch, seq, ATT_HEADS * ATT_VH), BF16),
        scratch_shapes=[pltpu.VMEM((n_chunks, acc_rows, tile), BF16)],
        compiler_params=pltpu.CompilerParams(
            dimension_semantics=("arbitrary", "arbitrary"),
            vmem_limit_bytes=_vmem_limit(est)),
        name="diff_attn",
    )(cfar, lamv, proj3d, proj3d, proj3d, bias_tiles, subln_col)


def _mix_ffn_kernel(x_ref, att_ref, b_ref, c_ref, h_ref, cp_ref, hp_ref, cn_ref, hn_ref, cw_ref, wo_ref,
                    g_ref, wg_ref, wu_ref, wd_ref, fg_ref, o_ref, gpad_scr, x1_scr, z_scr, hid_scr,
                    *, tiles_per_seq):
    tm, d = x_ref.shape
    att_w = att_ref.shape[1]
    d_ff = wg_ref.shape[1]
    pos = pl.program_id(0) % tiles_per_seq

    g = c_ref[...].astype(F32) * h_ref[...].astype(F32)
    g_prev = (cp_ref[...].astype(F32) * hp_ref[...].astype(F32))[BF16_SUBLANES - 1:BF16_SUBLANES]
    g_next = (cn_ref[...].astype(F32) * hn_ref[...].astype(F32))[0:1]
    g_prev = jnp.where(pos == 0, 0.0, g_prev)
    g_next = jnp.where(pos == tiles_per_seq - 1, 0.0, g_next)
    gpad_scr[7:8, :] = g_prev
    gpad_scr[8:8 + tm, :] = g
    gpad_scr[8 + tm:9 + tm, :] = g_next
    cw = cw_ref[...]
    hm = tm // MIX_ROW_SPLIT

    def mix(k):
        rows = slice(k * hm, (k + 1) * hm)
        lo = 8 + k * hm
        conv = b_ref[rows, :].astype(F32) * (cw[0:1] * gpad_scr[lo - 1:lo - 1 + hm, :]
                                             + cw[1:2] * gpad_scr[lo:lo + hm, :]
                                             + cw[2:3] * gpad_scr[lo + 1:lo + 1 + hm, :])
        conv = conv.astype(BF16)
        att = att_ref[rows, :]
        ss = jnp.zeros((hm, 1), F32)
        for n0 in range(0, d, PROJ_COL_CHUNK):
            cols = slice(n0, n0 + PROJ_COL_CHUNK)
            y = jnp.dot(att, wo_ref[0:att_w, cols], preferred_element_type=F32)
            y = y + jnp.dot(conv, wo_ref[att_w:, cols], preferred_element_type=F32)
            x1 = x_ref[rows, cols] + y
            x1_scr[rows, cols] = x1
            ss = ss + jnp.sum(x1 * x1, axis=-1, keepdims=True)
        z_scr[rows, :] = (x1_scr[rows, :] * lax.rsqrt(ss * (1.0 / d) + EPS) * g_ref[...]).astype(BF16)

    def gate_up(k):
        rows = slice(k * hm, (k + 1) * hm)
        for f0 in range(0, d_ff, FFN_COL_CHUNK):
            cols = slice(f0, f0 + FFN_COL_CHUNK)
            z = z_scr[rows, :]
            gate = jnp.dot(z, wg_ref[:, cols], preferred_element_type=F32)
            up = jnp.dot(z, wu_ref[:, cols], preferred_element_type=F32)
            hid_scr[rows, cols] = (gate * jax.nn.sigmoid(gate) * up).astype(BF16)

    def down(k):
        rows = slice(k * hm, (k + 1) * hm)
        ss = jnp.zeros((hm, 1), F32)
        for n0 in range(0, d, PROJ_COL_CHUNK):
            cols = slice(n0, n0 + PROJ_COL_CHUNK)
            y = x1_scr[rows, cols] + jnp.dot(hid_scr[rows, :], wd_ref[:, cols], preferred_element_type=F32)
            o_ref[rows, cols] = y
            ss = ss + jnp.sum(y * y, axis=-1, keepdims=True)
        return ss

    def final_norm(k, ss):
        rows = slice(k * hm, (k + 1) * hm)
        o_ref[rows, :] = o_ref[rows, :] * lax.rsqrt(ss * (1.0 / d) + EPS) * fg_ref[...]

    for k in range(MIX_ROW_SPLIT):
        mix(k)
    row_ss = None
    for k in range(MIX_ROW_SPLIT):
        gate_up(k)
        if k > 0:
            final_norm(k - 1, row_ss)
        row_ss = down(k)
    final_norm(MIX_ROW_SPLIT - 1, row_ss)


def _mix_ffn(x2d, att2d, proj2d, conv_w, wo, g, wg, wu, wd, fg, *, seq):
    tokens, d = x2d.shape
    att_w = att2d.shape[1]
    conv_width = conv_w.shape[1]
    d_ff = wg.shape[1]
    tm = ROW_TILE
    assert seq % tm == 0 and att_w == conv_width
    assert d % PROJ_COL_CHUNK == 0 and d_ff % FFN_COL_CHUNK == 0
    tiles_per_seq = seq // tm
    blk0 = (3 * att_w) // conv_width
    halo = BF16_SUBLANES
    rows_per_tile = tm // halo
    last_halo = tokens // halo - 1
    est = (4 * tm * d * 4 + 8 * tm * conv_width * 2 + (att_w + conv_width) * d * 2 + 3 * d * d_ff * 2
           + (tm + 16) * conv_width * 4 + tm * d * 4 + tm * (d + d_ff) * 2 + 6 * tm * PROJ_COL_CHUNK * 4)
    const = dict(pipeline_mode=pl.Buffered(1))
    prev_blk = lambda i: jnp.maximum(i * rows_per_tile - 1, 0)
    next_blk = lambda i: jnp.minimum((i + 1) * rows_per_tile, last_halo)
    return pl.pallas_call(
        functools.partial(_mix_ffn_kernel, tiles_per_seq=tiles_per_seq),
        grid=(tokens // tm,),
        in_specs=[
            pl.BlockSpec((tm, d), lambda i: (i, 0)),
            pl.BlockSpec((tm, att_w), lambda i: (i, 0)),
            pl.BlockSpec((tm, conv_width), lambda i: (i, blk0)),
            pl.BlockSpec((tm, conv_width), lambda i: (i, blk0 + 1)),
            pl.BlockSpec((tm, conv_width), lambda i: (i, blk0 + 2)),
            pl.BlockSpec((halo, conv_width), lambda i: (prev_blk(i), blk0 + 1)),
            pl.BlockSpec((halo, conv_width), lambda i: (prev_blk(i), blk0 + 2)),
            pl.BlockSpec((halo, conv_width), lambda i: (next_blk(i), blk0 + 1)),
            pl.BlockSpec((halo, conv_width), lambda i: (next_blk(i), blk0 + 2)),
            pl.BlockSpec(conv_w.shape, lambda i: (0, 0)),
            pl.BlockSpec(wo.shape, lambda i: (0, 0), **const),
            pl.BlockSpec((1, d), lambda i: (0, 0)),
            pl.BlockSpec((d, d_ff), lambda i: (0, 0), **const),
            pl.BlockSpec((d, d_ff), lambda i: (0, 0), **const),
            pl.BlockSpec((d_ff, d), lambda i: (0, 0), **const),
            pl.BlockSpec((1, d), lambda i: (0, 0)),
        ],
        out_specs=pl.BlockSpec((tm, d), lambda i: (i, 0)),
        out_shape=jax.ShapeDtypeStruct((tokens, d), F32),
        scratch_shapes=[
            pltpu.VMEM((tm + 16, conv_width), F32),
            pltpu.VMEM((tm, d), F32),
            pltpu.VMEM((tm, d), BF16),
            pltpu.VMEM((tm, d_ff), BF16),
        ],
        compiler_params=pltpu.CompilerParams(
            dimension_semantics=("arbitrary",), vmem_limit_bytes=_vmem_limit(est)),
        name="mix_ffn",
    )(x2d, att2d, proj2d, proj2d, proj2d, proj2d, proj2d, proj2d, proj2d, conv_w, wo, g, wg, wu, wd, fg)


def kernel(x, norm_mix_g, w_in, lambda_q1, lambda_k1, lambda_q2, lambda_k2, subln_g, rel_bias, conv_w,
           w_out, norm_ffn_g, w_gate, w_up, w_down, final_g):
    batch, seq, d = x.shape
    assert norm_mix_g.shape[0] == 1, "single-layer trunk"
    att_width = ATT_HEADS * ATT_VH
    tokens = batch * seq
    x2d = x.reshape(tokens, d)

    proj, (wo, wg, wu, wd) = _in_proj(
        x2d, norm_mix_g[0][None, :], w_in[0].astype(F32),
        [w_out[0].astype(F32), w_gate[0].astype(F32), w_up[0].astype(F32), w_down[0].astype(F32)],
        q_cols=att_width, q_scale=ATT_DH ** -0.5 * LOG2E)

    lamv = jnp.stack([lambda_q1[0], lambda_k1[0], lambda_q2[0], lambda_k2[0]], axis=0).astype(F32)
    att = _diff_attn(proj.reshape(batch, seq, proj.shape[1]), lamv, _bias_tiles(rel_bias, ATT_TILE),
                     subln_g[0].astype(F32)[:, None])

    out = _mix_ffn(x2d, att.reshape(tokens, att_width), proj, conv_w[0].astype(F32), wo,
                   norm_ffn_g[0][None, :], wg, wu, wd, final_g[None, :], seq=seq)
    return out.reshape(batch, seq, d)
```

```python
import functools
import math

import jax
import jax.numpy as jnp
from jax import lax
from jax.experimental import pallas as pl
from jax.experimental.pallas import tpu as pltpu

F32 = jnp.float32
BF16 = jnp.bfloat16

ATT_HEADS = 4
ATT_DH = 64
ATT_VH = 2 * ATT_DH
N_BUCKETS = 32
MAX_DISTANCE = 128
EPS = 1e-6
LAMBDA_INIT = 0.8 - 0.6 * math.exp(-0.3 * 0)
LOG2E = 1.4426950408889634

LANES = 128
BF16_SUBLANES = 16
VMEM_LIMIT_CAP = 56 * 1024 * 1024

ROW_TILE = 1024
IN_PROJ_ROW_TILE = 1024
MIX_ROW_SPLIT = 4
ATT_TILE = 256
PROJ_COL_CHUNK = 512
FFN_COL_CHUNK = 256
M_INIT = -1e30
SCORE_LOOKAHEAD = 3
TILES_PER_BODY = 16


def _vmem_limit(nbytes):
    return int(min(max(2 * nbytes, 16 * 1024 * 1024), VMEM_LIMIT_CAP))


def _t5_bucket(rel):
    n_half = N_BUCKETS // 2
    max_exact = n_half // 2
    ret = jnp.where(rel > 0, n_half, 0).astype(jnp.int32)
    n = jnp.abs(rel).astype(jnp.int32)
    nf = jnp.maximum(n, 1).astype(jnp.float32)
    large = max_exact + (jnp.log(nf / max_exact) / math.log(MAX_DISTANCE / max_exact)
                         * (n_half - max_exact)).astype(jnp.int32)
    large = jnp.minimum(large, n_half - 1)
    return ret + jnp.where(n < max_exact, n, large)


def _bias_tiles(rel_bias, tile):
    assert tile >= MAX_DISTANCE
    heads = rel_bias.shape[1]
    rel = jnp.arange(-2 * tile, 2 * tile, dtype=jnp.int32)
    table = rel_bias[_t5_bucket(rel)].astype(F32).T * LOG2E

    def toeplitz(dj):
        start = dj * tile - (tile - 1) + 2 * tile
        w = table[:, start:start + 2 * tile - 1]
        w = jnp.concatenate([w, jnp.zeros((heads, 1), F32)], axis=1)
        wrapped = jnp.tile(w, (1, tile))[:, :tile * (2 * tile - 1)].reshape(heads, tile, 2 * tile - 1)
        return jnp.swapaxes(wrapped[:, :, tile - 1:], 1, 2)

    def constant(value):
        return jnp.broadcast_to(value[:, None, None], (heads, tile, tile))

    return jnp.stack([constant(table[:, 0]), toeplitz(-1), toeplitz(0), toeplitz(1),
                      constant(table[:, -1])], axis=1)


def _in_proj_kernel(x_ref, g_ref, w_ref, *refs, q_cols, q_scale, n_side):
    side_in = refs[:n_side]
    o_ref = refs[n_side]
    side_out = refs[n_side + 1:2 * n_side + 1]
    wb_scr = refs[2 * n_side + 1]
    n_out = o_ref.shape[1]

    @pl.when(pl.program_id(0) == 0)
    def _():
        for n0 in range(0, n_out, PROJ_COL_CHUNK):
            wb_scr[:, n0:n0 + PROJ_COL_CHUNK] = w_ref[:, n0:n0 + PROJ_COL_CHUNK].astype(BF16)

    for src_ref, dst_ref in zip(side_in, side_out):
        dst_ref[...] = src_ref[...].astype(BF16)

    x = x_ref[...]
    ms = jnp.mean(x * x, axis=-1, keepdims=True)
    u = (x * lax.rsqrt(ms + EPS) * g_ref[...]).astype(BF16)
    for n0 in range(0, n_out, PROJ_COL_CHUNK):
        r = jnp.dot(u, wb_scr[:, n0:n0 + PROJ_COL_CHUNK], preferred_element_type=F32)
        if n0 < q_cols:
            r = r * q_scale
        o_ref[:, n0:n0 + PROJ_COL_CHUNK] = r.astype(BF16)


def _in_proj(x2d, g, w_f32, side_weights, *, q_cols, q_scale):
    tokens, d = x2d.shape
    n_out = w_f32.shape[1]
    tm = IN_PROJ_ROW_TILE
    steps = tokens // tm
    assert tokens % tm == 0 and q_cols % PROJ_COL_CHUNK == 0 and n_out % PROJ_COL_CHUNK == 0
    def block_rows(n_rows):
        rows = next(r for r in range(BF16_SUBLANES, n_rows + 1, BF16_SUBLANES)
                    if n_rows % r == 0 and n_rows // r <= steps)
        return rows

    side_rows = [block_rows(w.shape[0]) for w in side_weights]
    side_specs = [pl.BlockSpec((r, w.shape[1]), lambda i, last=w.shape[0] // r - 1: (jnp.minimum(i, last), 0))
                  for w, r in zip(side_weights, side_rows)]
    side_bytes = sum(r * w.shape[1] * 6 for w, r in zip(side_weights, side_rows))
    est = (2 * tm * d * 4 + d * n_out * 6 + 2 * tm * n_out * 2 + tm * (d * 2 + PROJ_COL_CHUNK * 8)
           + 2 * side_bytes)
    outs = pl.pallas_call(
        functools.partial(_in_proj_kernel, q_cols=q_cols, q_scale=q_scale, n_side=len(side_weights)),
        grid=(steps,),
        in_specs=[
            pl.BlockSpec((tm, d), lambda i: (i, 0)),
            pl.BlockSpec((1, d), lambda i: (0, 0)),
            pl.BlockSpec((d, n_out), lambda i: (0, 0), pipeline_mode=pl.Buffered(1)),
        ] + side_specs,
        out_specs=[pl.BlockSpec((tm, n_out), lambda i: (i, 0))] + side_specs,
        out_shape=[jax.ShapeDtypeStruct((tokens, n_out), BF16)]
        + [jax.ShapeDtypeStruct(w.shape, BF16) for w in side_weights],
        scratch_shapes=[pltpu.VMEM((d, n_out), BF16)],
        compiler_params=pltpu.CompilerParams(
            dimension_semantics=("arbitrary",), vmem_limit_bytes=_vmem_limit(est)),
        name="in_proj",
    )(x2d, g, w_f32, *side_weights)
    return outs[0], outs[1:]


def _diff_attn_kernel(cfar_ref, lamv_ref, q_ref, k_ref, v_ref, bias_ref, g_ref, o_ref, vt_scr, *, n_chunks):
    tile = ATT_TILE
    h = pl.program_id(1)

    for j in range(n_chunks):
        vc = v_ref[0, j * tile:(j + 1) * tile, :].astype(F32)
        vt_scr[j, 0:ATT_VH, :] = vc.T.astype(BF16)
        vt_scr[j, ATT_VH:ATT_VH + BF16_SUBLANES, :] = jnp.ones((BF16_SUBLANES, tile), BF16)

    lv = lamv_ref[...]
    lam = (jnp.exp(jnp.sum(lv[0:1] * lv[1:2], axis=-1, keepdims=True))
           - jnp.exp(jnp.sum(lv[2:3] * lv[3:4], axis=-1, keepdims=True)) + LAMBDA_INIT)
    c_left = cfar_ref[h, 0]
    c_right = cfar_ref[h, 1]

    def update(j, s2, offset, state):
        vt = vt_scr[j]
        new_state = []
        for c in range(2):
            m_old, acc = state[2 * c], state[2 * c + 1]
            s = s2[c]
            m_chunk = jnp.max(s, axis=0, keepdims=True).astype(F32) + offset
            shift = (jnp.maximum(m_old, m_chunk) - offset).astype(BF16)
            m_new = shift.astype(F32) + offset
            p = jnp.exp2(s - shift)
            pv = jnp.dot(vt, p, preferred_element_type=F32)
            new_state += [m_new, jnp.exp2(m_old - m_new) * acc + pv]
        return tuple(new_state)

    order = [0, 1, n_chunks - 1] + list(range(2, n_chunks - 1))
    m0 = jnp.full((1, tile), M_INIT, F32)
    acc0 = jnp.zeros((ATT_VH + BF16_SUBLANES, tile), F32)

    def query_tiles(it, carry):
        steps = [(t, idx) for t in range(TILES_PER_BODY) for idx in range(n_chunks)]
        ctx = {}

        def tile_ctx(t):
            if t not in ctx:
                i = it * TILES_PER_BODY + t
                rows = pl.ds(pl.multiple_of(i * tile, tile), tile)
                qt = q_ref[0, rows, :].astype(F32).T
                row = lax.broadcasted_iota(jnp.int32, qt.shape, 0)
                qts = (jnp.where(row < ATT_DH, qt, 0.0).astype(BF16),
                       jnp.where(row >= ATT_DH, qt, 0.0).astype(BF16))
                ctx[t] = (i, rows, qts, [lax.rem(i + r, n_chunks) for r in order])
            return ctx[t]

        def scores(step):
            i, _, qts, js = tile_ctx(step[0])
            j = js[step[1]]
            kc = k_ref[0, pl.ds(pl.multiple_of(j * tile, tile), tile), :]
            s2 = [jnp.dot(kc, qts[c], preferred_element_type=F32) for c in range(2)]
            if order[step[1]] in (0, 1, n_chunks - 1):
                bias = bias_ref[0, jnp.clip(j - i, -2, 2) + 2]
                s2 = [bias + s for s in s2]
            return [s.astype(BF16) for s in s2]

        pending = [scores(steps[d]) for d in range(SCORE_LOOKAHEAD)]
        state = None
        for n, (t, idx) in enumerate(steps):
            i, rows, _, js = tile_ctx(t)
            r = order[idx]
            if idx == 0:
                state = (m0, acc0, m0, acc0)
            s_cur = pending.pop(0)
            if n + SCORE_LOOKAHEAD < len(steps):
                pending.append(scores(steps[n + SCORE_LOOKAHEAD]))
            if r in (0, 1, n_chunks - 1):
                offset = 0.0
            else:
                offset = jnp.where(i + r < n_chunks, c_right, c_left)
            state = update(js[idx], s_cur, offset, state)
            if idx == n_chunks - 1:
                a1 = state[1]
                a2 = state[3]
                o = a1[0:ATT_VH] / a1[ATT_VH:ATT_VH + 1] - lam * (a2[0:ATT_VH] / a2[ATT_VH:ATT_VH + 1])
                ms = jnp.mean(o * o, axis=0, keepdims=True)
                y = o * lax.rsqrt(ms + EPS) * g_ref[...] * (1.0 - LAMBDA_INIT)
                o_ref[0, rows, :] = y.T.astype(BF16)
        return carry

    lax.fori_loop(0, n_chunks // TILES_PER_BODY, query_tiles, 0)


def _diff_attn(proj3d, lamv, bias_tiles, subln_col):
    batch, seq, _ = proj3d.shape
    tile = ATT_TILE
    assert seq % tile == 0 and 2 * ATT_DH == LANES and ATT_VH == LANES
    n_chunks = seq // tile
    assert n_chunks % TILES_PER_BODY == 0 and n_chunks >= 4
    acc_rows = ATT_VH + BF16_SUBLANES
    k_blk = ATT_HEADS
    v_blk = 2 * ATT_HEADS
    est = (8 * seq * LANES * 2 + 2 * 5 * tile * tile * 4 + n_chunks * acc_rows * tile * 2
           + 2 * acc_rows * tile * 4 + 2 * n_chunks * tile * tile * 4)
    cfar = jnp.stack([bias_tiles[:, 0, 0, 0], bias_tiles[:, 4, 0, 0]], axis=1)
    return pl.pallas_call(
        functools.partial(_diff_attn_kernel, n_chunks=n_chunks),
        grid=(batch, ATT_HEADS),
        in_specs=[
            pl.BlockSpec(memory_space=pltpu.SMEM),
            pl.BlockSpec((4, ATT_DH), lambda b, h: (0, 0)),
            pl.BlockSpec((1, seq, LANES), lambda b, h: (b, 0, h)),
            pl.BlockSpec((1, seq, LANES), lambda b, h: (b, 0, k_blk + h)),
            pl.BlockSpec((1, seq, LANES), lambda b, h: (b, 0, v_blk + h)),
            pl.BlockSpec((1, 5, tile, tile), lambda b, h: (h, 0, 0, 0)),
            pl.BlockSpec((ATT_VH, 1), lambda b, h: (0, 0)),
        ],
        out_specs=pl.BlockSpec((1, seq, LANES), lambda b, h: (b, 0, h)),
        out_shape=jax.ShapeDtypeStruct((batch, seq, ATT_HEADS * ATT_VH), BF16),
        scratch_shapes=[pltpu.VMEM((n_chunks, acc_rows, tile), BF16)],
        compiler_params=pltpu.CompilerParams(
            dimension_semantics=("arbitrary", "arbitrary"),
            vmem_limit_bytes=_vmem_limit(est)),
        name="diff_attn",
    )(cfar, lamv, proj3d, proj3d, proj3d, bias_tiles, subln_col)


def _mix_ffn_kernel(x_ref, att_ref, b_ref, c_ref, h_ref, cp_ref, hp_ref, cn_ref, hn_ref, cw_ref, wo_ref,
                    g_ref, wg_ref, wu_ref, wd_ref, fg_ref, o_ref, gpad_scr, z_scr, hid_scr,
                    *, tiles_per_seq):
    tm, d = x_ref.shape
    att_w = att_ref.shape[1]
    d_ff = wg_ref.shape[1]
    pos = pl.program_id(0) % tiles_per_seq

    g = c_ref[...].astype(F32) * h_ref[...].astype(F32)
    g_prev = (cp_ref[...].astype(F32) * hp_ref[...].astype(F32))[BF16_SUBLANES - 1:BF16_SUBLANES]
    g_next = (cn_ref[...].astype(F32) * hn_ref[...].astype(F32))[0:1]
    g_prev = jnp.where(pos == 0, 0.0, g_prev)
    g_next = jnp.where(pos == tiles_per_seq - 1, 0.0, g_next)
    gpad_scr[7:8, :] = g_prev
    gpad_scr[8:8 + tm, :] = g
    gpad_scr[8 + tm:9 + tm, :] = g_next
    cw = cw_ref[...]
    hm = tm // MIX_ROW_SPLIT

    def mix(k):
        rows = slice(k * hm, (k + 1) * hm)
        lo = 8 + k * hm
        conv = b_ref[rows, :].astype(F32) * (cw[0:1] * gpad_scr[lo - 1:lo - 1 + hm, :]
                                             + cw[1:2] * gpad_scr[lo:lo + hm, :]
                                             + cw[2:3] * gpad_scr[lo + 1:lo + 1 + hm, :])
        conv = conv.astype(BF16)
        att = att_ref[rows, :]
        ss = jnp.zeros((hm, 1), F32)
        for n0 in range(0, d, PROJ_COL_CHUNK):
            cols = slice(n0, n0 + PROJ_COL_CHUNK)
            y = jnp.dot(att, wo_ref[0:att_w, cols], preferred_element_type=F32)
            y = y + jnp.dot(conv, wo_ref[att_w:, cols], preferred_element_type=F32)
            x1 = x_ref[rows, cols] + y
            o_ref[rows, cols] = x1
            ss = ss + jnp.sum(x1 * x1, axis=-1, keepdims=True)
        z_scr[rows, :] = (o_ref[rows, :] * lax.rsqrt(ss * (1.0 / d) + EPS) * g_ref[...]).astype(BF16)

    def gate_up(k):
        rows = slice(k * hm, (k + 1) * hm)
        for f0 in range(0, d_ff, FFN_COL_CHUNK):
            cols = slice(f0, f0 + FFN_COL_CHUNK)
            z = z_scr[rows, :]
            gate = jnp.dot(z, wg_ref[:, cols], preferred_element_type=F32)
            up = jnp.dot(z, wu_ref[:, cols], preferred_element_type=F32)
            hid_scr[rows, cols] = (gate * jax.nn.sigmoid(gate) * up).astype(BF16)

    def down(k):
        rows = slice(k * hm, (k + 1) * hm)
        ss = jnp.zeros((hm, 1), F32)
        for n0 in range(0, d, PROJ_COL_CHUNK):
            cols = slice(n0, n0 + PROJ_COL_CHUNK)
            y = o_ref[rows, cols] + jnp.dot(hid_scr[rows, :], wd_ref[:, cols], preferred_element_type=F32)
            o_ref[rows, cols] = y
            ss = ss + jnp.sum(y * y, axis=-1, keepdims=True)
        return ss

    def final_norm(k, ss):
        rows = slice(k * hm, (k + 1) * hm)
        o_ref[rows, :] = o_ref[rows, :] * lax.rsqrt(ss * (1.0 / d) + EPS) * fg_ref[...]

    for k in range(MIX_ROW_SPLIT):
        mix(k)
    row_ss = None
    for k in range(MIX_ROW_SPLIT):
        gate_up(k)
        if k > 0:
            final_norm(k - 1, row_ss)
        row_ss = down(k)
    final_norm(MIX_ROW_SPLIT - 1, row_ss)


def _mix_ffn(x2d, att2d, proj2d, conv_w, wo, g, wg, wu, wd, fg, *, seq):
    tokens, d = x2d.shape
    att_w = att2d.shape[1]
    conv_width = conv_w.shape[1]
    d_ff = wg.shape[1]
    tm = ROW_TILE
    assert seq % tm == 0 and att_w == conv_width
    assert d % PROJ_COL_CHUNK == 0 and d_ff % FFN_COL_CHUNK == 0
    tiles_per_seq = seq // tm
    blk0 = (3 * att_w) // conv_width
    halo = BF16_SUBLANES
    rows_per_tile = tm // halo
    last_halo = tokens // halo - 1
    est = (4 * tm * d * 4 + 8 * tm * conv_width * 2 + (att_w + conv_width) * d * 2 + 3 * d * d_ff * 2
           + (tm + 16) * conv_width * 4 + tm * (d + d_ff) * 2 + 6 * tm * PROJ_COL_CHUNK * 4)
    const = dict(pipeline_mode=pl.Buffered(1))
    prev_blk = lambda i: jnp.maximum(i * rows_per_tile - 1, 0)
    next_blk = lambda i: jnp.minimum((i + 1) * rows_per_tile, last_halo)
    return pl.pallas_call(
        functools.partial(_mix_ffn_kernel, tiles_per_seq=tiles_per_seq),
        grid=(tokens // tm,),
        in_specs=[
            pl.BlockSpec((tm, d), lambda i: (i, 0)),
            pl.BlockSpec((tm, att_w), lambda i: (i, 0)),
            pl.BlockSpec((tm, conv_width), lambda i: (i, blk0)),
            pl.BlockSpec((tm, conv_width), lambda i: (i, blk0 + 1)),
            pl.BlockSpec((tm, conv_width), lambda i: (i, blk0 + 2)),
            pl.BlockSpec((halo, conv_width), lambda i: (prev_blk(i), blk0 + 1)),
            pl.BlockSpec((halo, conv_width), lambda i: (prev_blk(i), blk0 + 2)),
            pl.BlockSpec((halo, conv_width), lambda i: (next_blk(i), blk0 + 1)),
            pl.BlockSpec((halo, conv_width), lambda i: (next_blk(i), blk0 + 2)),
            pl.BlockSpec(conv_w.shape, lambda i: (0, 0)),
            pl.BlockSpec(wo.shape, lambda i: (0, 0), **const),
            pl.BlockSpec((1, d), lambda i: (0, 0)),
            pl.BlockSpec((d, d_ff), lambda i: (0, 0), **const),
            pl.BlockSpec((d, d_ff), lambda i: (0, 0), **const),
            pl.BlockSpec((d_ff, d), lambda i: (0, 0), **const),
            pl.BlockSpec((1, d), lambda i: (0, 0)),
        ],
        out_specs=pl.BlockSpec((tm, d), lambda i: (i, 0)),
        out_shape=jax.ShapeDtypeStruct((tokens, d), F32),
        scratch_shapes=[
            pltpu.VMEM((tm + 16, conv_width), F32),
            pltpu.VMEM((tm, d), BF16),
            pltpu.VMEM((tm, d_ff), BF16),
        ],
        compiler_params=pltpu.CompilerParams(
            dimension_semantics=("arbitrary",), vmem_limit_bytes=_vmem_limit(est)),
        name="mix_ffn",
    )(x2d, att2d, proj2d, proj2d, proj2d, proj2d, proj2d, proj2d, proj2d, conv_w, wo, g, wg, wu, wd, fg)


def kernel(x, norm_mix_g, w_in, lambda_q1, lambda_k1, lambda_q2, lambda_k2, subln_g, rel_bias, conv_w,
           w_out, norm_ffn_g, w_gate, w_up, w_down, final_g):
    batch, seq, d = x.shape
    assert norm_mix_g.shape[0] == 1, "single-layer trunk"
    att_width = ATT_HEADS * ATT_VH
    tokens = batch * seq
    x2d = x.reshape(tokens, d)

    proj, (wo, wg, wu, wd) = _in_proj(
        x2d, norm_mix_g[0][None, :], w_in[0].astype(F32),
        [w_out[0].astype(F32), w_gate[0].astype(F32), w_up[0].astype(F32), w_down[0].astype(F32)],
        q_cols=att_width, q_scale=ATT_DH ** -0.5 * LOG2E)

    lamv = jnp.stack([lambda_q1[0], lambda_k1[0], lambda_q2[0], lambda_k2[0]], axis=0).astype(F32)
    att = _diff_attn(proj.reshape(batch, seq, proj.shape[1]), lamv, _bias_tiles(rel_bias, ATT_TILE),
                     subln_g[0].astype(F32)[:, None])

    out = _mix_ffn(x2d, att.reshape(tokens, att_width), proj, conv_w[0].astype(F32), wo,
                   norm_ffn_g[0][None, :], wg, wu, wd, final_g[None, :], seq=seq)
    return out.reshape(batch, seq, d)
```
